```python
import math
import jax, jax.numpy as jnp
from jax import lax
import numpy as np

D_MODEL = 1024
BATCH = 4
SEQ = 4096
DEPTH = 4

N_MEM = 256
HEAD_DIM = 64
MIX_WIDTH = D_MODEL
MEM_HEADS = 4
MEM_WIDTH = MEM_HEADS * HEAD_DIM
MAIN_WIDTH = MIX_WIDTH - MEM_WIDTH
SB_HEADS = MAIN_WIDTH // HEAD_DIM
CONV_WIDTH = 3
D_FF = -(-8 * D_MODEL // (3 * 256)) * 256
N_A_LAYERS = DEPTH // 2
N_B_LAYERS = DEPTH - N_A_LAYERS
BLOCK_Q = 128
EPS = 1e-6

kernel_name = "shortconv_stickbreaking_yoco_hybrid"


def rmsnorm(x, g):
    xf = x.astype(jnp.float32)
    y = xf * lax.rsqrt(jnp.mean(xf * xf, axis=-1, keepdims=True) + EPS)
    return (y * g.astype(jnp.float32)).astype(x.dtype)


def causal_short_conv(u, w):
    c = u.shape[-1]
    return lax.conv_general_dilated(
        u, w[:, None, :].astype(u.dtype), window_strides=(1,),
        padding=[(CONV_WIDTH - 1, 0)],
        dimension_numbers=("NWC", "WIO", "NWC"),
        feature_group_count=c)


def memory_cross_attention(q, mem_k, mem_v):
    scale = 1.0 / math.sqrt(HEAD_DIM)
    s = jnp.einsum("bshd,bmhd->bhsm", q.astype(jnp.float32), mem_k.astype(jnp.float32)) * scale
    p = jax.nn.softmax(s, axis=-1)
    o = jnp.einsum("bhsm,bmhd->bshd", p, mem_v.astype(jnp.float32))
    return o.astype(q.dtype)


def stick_breaking_attention(q, k, v):
    b, s_len, h, d = q.shape
    scale = 1.0 / math.sqrt(d)
    qh = jnp.transpose(q, (0, 2, 1, 3)).astype(jnp.float32)
    kh = jnp.transpose(k, (0, 2, 1, 3)).astype(jnp.float32)
    vh = jnp.transpose(v, (0, 2, 1, 3)).astype(jnp.float32)
    outs = []
    for blk in range(s_len // BLOCK_Q):
        start = blk * BLOCK_Q
        end = start + BLOCK_Q
        qb = qh[:, :, start:end]
        kb = kh[:, :, :end]
        vb = vh[:, :, :end]
        z = jnp.einsum("bhtd,bhsd->bhts", qb, kb) * scale
        t_idx = start + jnp.arange(BLOCK_Q)[:, None]
        s_idx = jnp.arange(end)[None, :]
        causal = s_idx < t_idx
        log_not = jnp.where(causal, jax.nn.log_sigmoid(-z), 0.0)
        tail = lax.cumsum(log_not, axis=3, reverse=True) - log_not
        log_a = jax.nn.log_sigmoid(z) + tail
        a = jnp.where(causal, jnp.exp(log_a), 0.0)
        outs.append(jnp.einsum("bhts,bhsd->bhtd", a, vb))
    o = jnp.concatenate(outs, axis=2)
    return jnp.transpose(o, (0, 2, 1, 3)).astype(q.dtype)


def swiglu(h, w_gate, w_up, w_down):
    return (jax.nn.silu(h @ w_gate) * (h @ w_up)) @ w_down


def setup_inputs(seed: int = 0) -> dict:
    key = jax.random.key(seed)
    ks = jax.random.split(key, 16)
    f32 = jnp.float32

    def nrm(k, shape, fan_in):
        return jax.random.normal(k, shape, f32) * (fan_in ** -0.5)

    def gain(k, shape):
        return jnp.ones(shape, f32) + 0.02 * jax.random.normal(k, shape, f32)

    x = jax.random.normal(ks[0], (BATCH, SEQ, D_MODEL), f32)
    mem = jax.random.normal(ks[1], (BATCH, N_MEM, D_MODEL), f32)
    return {
        "x": x,
        "mem": mem,
        "mix_norm": gain(ks[2], (DEPTH, D_MODEL)),
        "a_in": nrm(ks[3], (N_A_LAYERS, D_MODEL, 3 * MAIN_WIDTH + MEM_WIDTH), D_MODEL),
        "conv_w": nrm(ks[4], (N_A_LAYERS, CONV_WIDTH, MAIN_WIDTH), CONV_WIDTH),
        "b_in": nrm(ks[5], (N_B_LAYERS, D_MODEL, MAIN_WIDTH + MEM_WIDTH), D_MODEL),
        "kv_norm": gain(ks[6], (D_MODEL,)),
        "w_kv_shared": nrm(ks[7], (D_MODEL, 2 * MAIN_WIDTH), D_MODEL),
        "w_mem_kv": nrm(ks[8], (DEPTH, D_MODEL, 2 * MEM_WIDTH), D_MODEL),
        "w_o": nrm(ks[9], (DEPTH, MIX_WIDTH, D_MODEL), MIX_WIDTH),
        "ffn_norm": gain(ks[10], (DEPTH, D_MODEL)),
        "w_gate": nrm(ks[11], (DEPTH, D_MODEL, D_FF), D_MODEL),
        "w_up": nrm(ks[12], (DEPTH, D_MODEL, D_FF), D_MODEL),
        "w_down": nrm(ks[13], (DEPTH, D_FF, D_MODEL), D_FF),
        "mem_norm": gain(ks[14], (D_MODEL,)),
        "final_norm": gain(ks[15], (D_MODEL,)),
    }


def reference(x, mem, mix_norm, a_in, conv_w, b_in, kv_norm, w_kv_shared, w_mem_kv,
              w_o, ffn_norm, w_gate, w_up, w_down, mem_norm, final_norm):
    b, s_len, _ = x.shape
    m_len = mem.shape[1]
    mem_n = rmsnorm(mem, mem_norm)
    k_sh = None
    v_sh = None
    for i in range(DEPTH):
        h = rmsnorm(x, mix_norm[i])
        mkv = (mem_n @ w_mem_kv[i]).reshape(b, m_len, 2, MEM_HEADS, HEAD_DIM)
        mem_k, mem_v = mkv[:, :, 0], mkv[:, :, 1]
        if i < N_A_LAYERS:
            p = h @ a_in[i]
            b_gate = p[..., :MAIN_WIDTH]
            c_gate = p[..., MAIN_WIDTH:2 * MAIN_WIDTH]
            u = p[..., 2 * MAIN_WIDTH:3 * MAIN_WIDTH]
            q_mem = p[..., 3 * MAIN_WIDTH:]
            y_main = b_gate * causal_short_conv(c_gate * u, conv_w[i])
        else:
            j = i - N_A_LAYERS
            p = h @ b_in[j]
            q_sb = p[..., :MAIN_WIDTH].reshape(b, s_len, SB_HEADS, HEAD_DIM)
            q_mem = p[..., MAIN_WIDTH:]
            y_main = stick_breaking_attention(q_sb, k_sh, v_sh).reshape(b, s_len, MAIN_WIDTH)
        y_mem = memory_cross_attention(
            q_mem.reshape(b, s_len, MEM_HEADS, HEAD_DIM), mem_k, mem_v
        ).reshape(b, s_len, MEM_WIDTH)
        x = x + jnp.concatenate([y_main, y_mem], axis=-1) @ w_o[i]
        x = x + swiglu(rmsnorm(x, ffn_norm[i]), w_gate[i], w_up[i], w_down[i])
        if i == N_A_LAYERS - 1:
            kv = (rmsnorm(x, kv_norm) @ w_kv_shared).reshape(b, s_len, 2, SB_HEADS, HEAD_DIM)
            k_sh, v_sh = kv[:, :, 0], kv[:, :, 1]
    return rmsnorm(x, final_norm)
```

```python
from functools import partial

import jax
import jax.numpy as jnp
from jax import lax
from jax.experimental import pallas as pl
from jax.experimental.pallas import tpu as pltpu

HEAD_DIM = 64
MEM_HEADS = 4
MEM_WIDTH = MEM_HEADS * HEAD_DIM
CONV_WIDTH = 3
EPS = 1e-6

V7X_LANES = 128
V7X_SUBLANES = 8
V7X_VMEM_LIMIT_BYTES = 56 * 1024 * 1024

ROW_TILE = 512
SB_TILE = 256
FFN_CHUNKS = 2

_NT = (((1,), (1,)), ((), ()))


def _rmsnorm(x, g):
    y = x * lax.rsqrt(jnp.mean(x * x, axis=-1, keepdims=True) + EPS)
    return y * g


def _dot(a, b):
    return jnp.dot(a, b, preferred_element_type=jnp.float32)


def _resident(shape):
    return pl.BlockSpec(shape, lambda *_: (0,) * len(shape), pipeline_mode=pl.Buffered(1))


def _params(semantics):
    return pltpu.CompilerParams(dimension_semantics=semantics,
                                vmem_limit_bytes=V7X_VMEM_LIMIT_BYTES)


def _norm_matmul_kernel(x_ref, g_ref, w_ref, o_ref):
    h = _rmsnorm(x_ref[...], g_ref[...]).astype(jnp.bfloat16)
    o_ref[...] = _dot(h, w_ref[...]).astype(o_ref.dtype)


def _norm_matmul(x, g, w):
    m, d = x.shape
    n = w.shape[1]
    tm = min(ROW_TILE, m)
    return pl.pallas_call(
        _norm_matmul_kernel,
        grid=(m // tm,),
        in_specs=[pl.BlockSpec((tm, d), lambda i: (i, 0)),
                  _resident((1, d)),
                  _resident((d, n))],
        out_specs=pl.BlockSpec((tm, n), lambda i: (i, 0)),
        out_shape=jax.ShapeDtypeStruct((m, n), jnp.bfloat16),
        compiler_params=_params(("parallel",)),
        name="norm_matmul",
    )(x, g.reshape(1, d), w)


def _memory_attention(q_mem, mk, mv):
    lane = lax.broadcasted_iota(jnp.int32, (1, MEM_WIDTH), 1)
    scale = 1.0 / (HEAD_DIM ** 0.5)
    out = jnp.zeros(q_mem.shape, jnp.float32)
    for h in range(MEM_HEADS):
        in_head = (lane >= h * HEAD_DIM) & (lane < (h + 1) * HEAD_DIM)
        qh = jnp.where(in_head, q_mem, 0.0).astype(jnp.bfloat16)
        s = lax.dot_general(qh, mk, _NT, preferred_element_type=jnp.float32) * scale
        p = jnp.exp(s - jnp.max(s, axis=-1, keepdims=True))
        l = jnp.sum(p, axis=-1, keepdims=True)
        vh = jnp.where(in_head, mv, jnp.zeros_like(mv))
        out = out + _dot(p.astype(jnp.bfloat16), vh) / l
    return out


def _out_projection(x, y_main, y_mem, wo_ref):
    main_width = y_main.shape[1]
    y = _dot(y_main.astype(jnp.bfloat16), wo_ref[:main_width, :])
    y = y + _dot(y_mem.astype(jnp.bfloat16), wo_ref[main_width:, :])
    return x + y


def _conv_mixer_kernel(x_ref, g_ref, win_ref, cw_ref, mk_ref, mv_ref, wo_ref, o_ref, tail_ref):
    main = cw_ref.shape[1]
    x = x_ref[...]
    h = _rmsnorm(x, g_ref[...]).astype(jnp.bfloat16)
    p = _dot(h, win_ref[...])
    b_gate = p[:, :main]
    cu = p[:, main:2 * main] * p[:, 2 * main:3 * main]
    q_mem = p[:, 3 * main:]

    @pl.when(pl.program_id(1) == 0)
    def _():
        tail_ref[...] = jnp.zeros_like(tail_ref)

    prev = tail_ref[...]
    rows = lax.broadcasted_iota(jnp.int32, prev.shape, 0)
    conv = cu * cw_ref[CONV_WIDTH - 1:CONV_WIDTH, :]
    for back in range(1, CONV_WIDTH):
        shifted = pltpu.roll(cu, back, 0)
        top = jnp.where(rows < back, pltpu.roll(prev, back, 0), shifted[:V7X_SUBLANES])
        shifted = jnp.concatenate([top, shifted[V7X_SUBLANES:]], axis=0)
        conv = conv + shifted * cw_ref[CONV_WIDTH - 1 - back:CONV_WIDTH - back, :]
    tail_ref[...] = cu[cu.shape[0] - V7X_SUBLANES:]

    y_main = b_gate * conv
    y_mem = _memory_attention(q_mem, mk_ref[...], mv_ref[...])
    o_ref[...] = _out_projection(x, y_main, y_mem, wo_ref)


def _conv_mixer(x, g, w_in, conv_w, mkv, layer, w_o, batch):
    m, d = x.shape
    n_in = w_in.shape[1]
    main = conv_w.shape[1]
    seq = m // batch
    tm = ROW_TILE
    nt = seq // tm
    n_mem = mkv.shape[0] // batch
    return pl.pallas_call(
        _conv_mixer_kernel,
        grid=(batch, nt),
        in_specs=[pl.BlockSpec((tm, d), lambda b, t: (b * nt + t, 0)),
                  _resident((1, d)),
                  _resident((d, n_in)),
                  _resident((CONV_WIDTH, main)),
                  pl.BlockSpec((n_mem, MEM_WIDTH), lambda b, t: (b, 2 * layer)),
                  pl.BlockSpec((n_mem, MEM_WIDTH), lambda b, t: (b, 2 * layer + 1)),
                  _resident((d, d))],
        out_specs=pl.BlockSpec((tm, d), lambda b, t: (b * nt + t, 0)),
        out_shape=jax.ShapeDtypeStruct((m, d), jnp.float32),
        scratch_shapes=[pltpu.VMEM((V7X_SUBLANES, main), jnp.float32)],
        compiler_params=_params(("arbitrary", "arbitrary")),
        name="conv_mixer",
    )(x, g.reshape(1, d), w_in, conv_w, mkv, mkv, w_o)


def _softplus(z):
    return jnp.maximum(z, 0.0) + jnp.log(1.0 + jnp.exp(-jnp.abs(z)))


def _split_bf16(x):
    hi = x.astype(jnp.bfloat16)
    lo = (x - hi.astype(jnp.float32)).astype(jnp.bfloat16)
    return jnp.concatenate([hi, lo], axis=1)


def _sb_attention_kernel(q_ref, k_ref, v_ref, o_ref):
    tq = q_ref.shape[0]
    tk = tq
    qi = pl.program_id(2)
    q = q_ref[...]
    lane = lax.broadcasted_iota(jnp.int32, (1, V7X_LANES), 1)
    r = lax.broadcasted_iota(jnp.int32, (tk, tk), 0)
    s = lax.broadcasted_iota(jnp.int32, (tk, tk), 1)
    strict = r > s
    suffix = jnp.where(strict, 1.0, 0.0).astype(jnp.bfloat16)
    suffix2 = jnp.concatenate([suffix, suffix], axis=0)

    def block(qh, j, c, acc, causal):
        kb = k_ref[pl.ds(pl.multiple_of(j * tk, tk), tk), :]
        vb = v_ref[pl.ds(pl.multiple_of(j * tk, tk), tk), :]
        z = lax.dot_general(qh, kb, _NT, preferred_element_type=jnp.float32)
        p = _softplus(z)
        if causal is not None:
            p = jnp.where(causal, p, 0.0)
        inner = _dot(_split_bf16(p), suffix2)
        a = jnp.exp(z - p - inner - c)
        if causal is not None:
            a = jnp.where(causal, a, 0.0)
        acc = acc + _dot(a.astype(jnp.bfloat16), vb)
        c = c + jnp.sum(p, axis=1, keepdims=True)
        return c, acc

    out = jnp.zeros((tq, V7X_LANES), jnp.float32)
    for hh in range(V7X_LANES // HEAD_DIM):
        in_head = (lane >= hh * HEAD_DIM) & (lane < (hh + 1) * HEAD_DIM)
        qh = jnp.where(in_head, q, jnp.zeros_like(q)) * (HEAD_DIM ** -0.5)
        c0 = jnp.zeros((tq, 1), jnp.float32)
        acc0 = jnp.zeros((tq, V7X_LANES), jnp.float32)
        c1, acc1 = block(qh, qi, c0, acc0, strict)

        def body(step, carry):
            return block(qh, qi - 1 - step, carry[0], carry[1], None)

        _, acc = lax.fori_loop(0, qi, body, (c1, acc1))
        out = jnp.where(in_head, acc, out)
    o_ref[...] = out.astype(o_ref.dtype)


def _sb_attention(p, kv, batch, main):
    m = p.shape[0]
    seq = m // batch
    nq = seq // SB_TILE
    n_pairs = main // V7X_LANES
    return pl.pallas_call(
        _sb_attention_kernel,
        grid=(batch, n_pairs, nq),
        in_specs=[pl.BlockSpec((SB_TILE, V7X_LANES), lambda b, hp, qi: (b * nq + qi, hp)),
                  pl.BlockSpec((seq, V7X_LANES), lambda b, hp, qi: (b, hp)),
                  pl.BlockSpec((seq, V7X_LANES), lambda b, hp, qi: (b, n_pairs + hp))],
        out_specs=pl.BlockSpec((SB_TILE, V7X_LANES), lambda b, hp, qi: (b * nq + qi, hp)),
        out_shape=jax.ShapeDtypeStruct((m, main), jnp.bfloat16),
        compiler_params=_params(("parallel", "parallel", "parallel")),
        name="sb_attention",
    )(p, kv, kv)


def _attn_out_kernel(x_ref, ymain_ref, qmem_ref, mk_ref, mv_ref, wo_ref, o_ref):
    y_mem = _memory_attention(qmem_ref[...].astype(jnp.float32), mk_ref[...], mv_ref[...])
    o_ref[...] = _out_projection(x_ref[...], ymain_ref[...], y_mem, wo_ref)


def _attn_out(x, y_main, p, mkv, layer, w_o, batch):
    m, d = x.shape
    main = y_main.shape[1]
    tm = ROW_TILE
    nt = m // batch // tm
    n_mem = mkv.shape[0] // batch
    return pl.pallas_call(
        _attn_out_kernel,
        grid=(batch, nt),
        in_specs=[pl.BlockSpec((tm, d), lambda b, t: (b * nt + t, 0)),
                  pl.BlockSpec((tm, main), lambda b, t: (b * nt + t, 0)),
                  pl.BlockSpec((tm, MEM_WIDTH), lambda b, t: (b * nt + t, main // MEM_WIDTH)),
                  pl.BlockSpec((n_mem, MEM_WIDTH), lambda b, t: (b, 2 * layer)),
                  pl.BlockSpec((n_mem, MEM_WIDTH), lambda b, t: (b, 2 * layer + 1)),
                  _resident((d, d))],
        out_specs=pl.BlockSpec((tm, d), lambda b, t: (b * nt + t, 0)),
        out_shape=jax.ShapeDtypeStruct((m, d), jnp.float32),
        compiler_params=_params(("parallel", "parallel")),
        name="attn_out",
    )(x, y_main, p, mkv, mkv, w_o)


def _ffn_kernel(x_ref, g_ref, wg_ref, wu_ref, wd_ref, fg_ref, o_ref, *, final_norm):
    x = x_ref[...]
    h = _rmsnorm(x, g_ref[...]).astype(jnp.bfloat16)
    d_ff = wg_ref.shape[1]
    chunk = d_ff // FFN_CHUNKS
    y = x
    for c in range(FFN_CHUNKS):
        cols = slice(c * chunk, (c + 1) * chunk)
        gate = _dot(h, wg_ref[:, cols])
        up = _dot(h, wu_ref[:, cols])
        act = (gate / (1.0 + jnp.exp(-gate))) * up
        y = y + _dot(act.astype(jnp.bfloat16), wd_ref[cols, :])
    if final_norm:
        y = _rmsnorm(y, fg_ref[...])
    o_ref[...] = y


def _ffn(x, g, w_gate, w_up, w_down, final_g, final_norm):
    m, d = x.shape
    d_ff = w_gate.shape[1]
    assert d_ff % (FFN_CHUNKS * V7X_LANES) == 0
    tm = ROW_TILE
    return pl.pallas_call(
        partial(_ffn_kernel, final_norm=final_norm),
        grid=(m // tm,),
        in_specs=[pl.BlockSpec((tm, d), lambda i: (i, 0)),
                  _resident((1, d)),
                  _resident((d, d_ff)),
                  _resident((d, d_ff)),
                  _resident((d_ff, d)),
                  _resident((1, d))],
        out_specs=pl.BlockSpec((tm, d), lambda i: (i, 0)),
        out_shape=jax.ShapeDtypeStruct((m, d), jnp.float32),
        compiler_params=_params(("parallel",)),
        name="ffn",
    )(x, g.reshape(1, d), w_gate, w_up, w_down, final_g.reshape(1, d))


def kernel(x, mem, mix_norm, a_in, conv_w, b_in, kv_norm, w_kv_shared, w_mem_kv, w_o, ffn_norm,
           w_gate, w_up, w_down, mem_norm, final_norm):
    batch, seq, d = x.shape
    depth = mix_norm.shape[0]
    n_a = a_in.shape[0]
    main = conv_w.shape[2]
    bf16 = jnp.bfloat16
    assert seq % ROW_TILE == 0 and seq % SB_TILE == 0 and main % V7X_LANES == 0

    xs = x.reshape(batch * seq, d)
    w_mkv = jnp.transpose(w_mem_kv, (1, 0, 2)).reshape(d, depth * 2 * MEM_WIDTH).astype(bf16)
    mkv = _norm_matmul(mem.reshape(-1, d), mem_norm, w_mkv)

    kv = None
    for i in range(depth):
        wo_i = w_o[i].astype(bf16)
        if i < n_a:
            xs = _conv_mixer(xs, mix_norm[i], a_in[i].astype(bf16), conv_w[i], mkv, i, wo_i, batch)
        else:
            p = _norm_matmul(xs, mix_norm[i], b_in[i - n_a].astype(bf16))
            y_main = _sb_attention(p, kv, batch, main)
            xs = _attn_out(xs, y_main, p, mkv, i, wo_i, batch)
        last = i == depth - 1
        xs = _ffn(xs, ffn_norm[i], w_gate[i].astype(bf16), w_up[i].astype(bf16),
                  w_down[i].astype(bf16), final_norm, last)
        if i == n_a - 1:
            kv = _norm_matmul(xs, kv_norm, w_kv_shared.astype(bf16))
    return xs.reshape(batch, seq, d)
```

```python
from functools import partial

import jax
import jax.numpy as jnp
from jax import lax
from jax.experimental import pallas as pl
from jax.experimental.pallas import tpu as pltpu

HEAD_DIM = 64
MEM_HEADS = 4
MEM_WIDTH = MEM_HEADS * HEAD_DIM
CONV_WIDTH = 3
EPS = 1e-6

V7X_LANES = 128
V7X_SUBLANES = 8
V7X_VMEM_LIMIT_BYTES = 56 * 1024 * 1024

ROW_TILE = 512
SB_TILE = 256
FFN_CHUNKS = 2

_NT = (((1,), (1,)), ((), ()))


def _rmsnorm(x, g):
    y = x * lax.rsqrt(jnp.mean(x * x, axis=-1, keepdims=True) + EPS)
    return y * g


def _dot(a, b):
    return jnp.dot(a, b, preferred_element_type=jnp.float32)


def _resident(shape):
    return pl.BlockSpec(shape, lambda *_: (0,) * len(shape), pipeline_mode=pl.Buffered(1))


def _params(semantics):
    return pltpu.CompilerParams(dimension_semantics=semantics,
                                vmem_limit_bytes=V7X_VMEM_LIMIT_BYTES)


def _norm_matmul_kernel(x_ref, g_ref, w_ref, o_ref):
    h = _rmsnorm(x_ref[...], g_ref[...]).astype(jnp.bfloat16)
    o_ref[...] = _dot(h, w_ref[...]).astype(o_ref.dtype)


def _norm_matmul(x, g, w):
    m, d = x.shape
    n = w.shape[1]
    tm = min(ROW_TILE, m)
    return pl.pallas_call(
        _norm_matmul_kernel,
        grid=(m // tm,),
        in_specs=[pl.BlockSpec((tm, d), lambda i: (i, 0)),
                  _resident((1, d)),
                  _resident((d, n))],
        out_specs=pl.BlockSpec((tm, n), lambda i: (i, 0)),
        out_shape=jax.ShapeDtypeStruct((m, n), jnp.bfloat16),
        compiler_params=_params(("parallel",)),
        name="norm_matmul",
    )(x, g.reshape(1, d), w)


def _memory_attention(q_mem, mk, mv):
    lane = lax.broadcasted_iota(jnp.int32, (1, MEM_WIDTH), 1)
    scale = 1.0 / (HEAD_DIM ** 0.5)
    out = jnp.zeros(q_mem.shape, jnp.float32)
    for h in range(MEM_HEADS):
        in_head = (lane >= h * HEAD_DIM) & (lane < (h + 1) * HEAD_DIM)
        qh = jnp.where(in_head, q_mem, 0.0).astype(jnp.bfloat16)
        s = lax.dot_general(qh, mk, _NT, preferred_element_type=jnp.float32) * scale
        p = jnp.exp(s - jnp.max(s, axis=-1, keepdims=True))
        l = jnp.sum(p, axis=-1, keepdims=True)
        vh = jnp.where(in_head, mv, jnp.zeros_like(mv))
        out = out + _dot(p.astype(jnp.bfloat16), vh) / l
    return out


def _out_projection(x, y_main, y_mem, wo_ref):
    main_width = y_main.shape[1]
    y = _dot(y_main.astype(jnp.bfloat16), wo_ref[:main_width, :])
    y = y + _dot(y_mem.astype(jnp.bfloat16), wo_ref[main_width:, :])
    return x + y


def _conv_mixer_kernel(x_ref, g_ref, win_ref, cw_ref, mk_ref, mv_ref, wo_ref, o_ref, tail_ref):
    main = cw_ref.shape[1]
    x = x_ref[...]
    h = _rmsnorm(x, g_ref[...]).astype(jnp.bfloat16)
    p = _dot(h, win_ref[...])
    b_gate = p[:, :main]
    cu = p[:, main:2 * main] * p[:, 2 * main:3 * main]
    q_mem = p[:, 3 * main:]

    @pl.when(pl.program_id(1) == 0)
    def _():
        tail_ref[...] = jnp.zeros_like(tail_ref)

    prev = tail_ref[...]
    rows = lax.broadcasted_iota(jnp.int32, prev.shape, 0)
    conv = cu * cw_ref[CONV_WIDTH - 1:CONV_WIDTH, :]
    for back in range(1, CONV_WIDTH):
        shifted = pltpu.roll(cu, back, 0)
        top = jnp.where(rows < back, pltpu.roll(prev, back, 0), shifted[:V7X_SUBLANES])
        shifted = jnp.concatenate([top, shifted[V7X_SUBLANES:]], axis=0)
        conv = conv + shifted * cw_ref[CONV_WIDTH - 1 - back:CONV_WIDTH - back, :]
    tail_ref[...] = cu[cu.shape[0] - V7X_SUBLANES:]

    y_main = b_gate * conv
    y_mem = _memory_attention(q_mem, mk_ref[...], mv_ref[...])
    o_ref[...] = _out_projection(x, y_main, y_mem, wo_ref)


def _conv_mixer(x, g, w_in, conv_w, mkv, layer, w_o, batch):
    m, d = x.shape
    n_in = w_in.shape[1]
    main = conv_w.shape[1]
    seq = m // batch
    tm = ROW_TILE
    nt = seq // tm
    n_mem = mkv.shape[0] // batch
    return pl.pallas_call(
        _conv_mixer_kernel,
        grid=(batch, nt),
        in_specs=[pl.BlockSpec((tm, d), lambda b, t: (b * nt + t, 0)),
                  _resident((1, d)),
                  _resident((d, n_in)),
                  _resident((CONV_WIDTH, main)),
                  pl.BlockSpec((n_mem, MEM_WIDTH), lambda b, t: (b, 2 * layer)),
                  pl.BlockSpec((n_mem, MEM_WIDTH), lambda b, t: (b, 2 * layer + 1)),
                  _resident((d, d))],
        out_specs=pl.BlockSpec((tm, d), lambda b, t: (b * nt + t, 0)),
        out_shape=jax.ShapeDtypeStruct((m, d), jnp.float32),
        scratch_shapes=[pltpu.VMEM((V7X_SUBLANES, main), jnp.float32)],
        compiler_params=_params(("arbitrary", "arbitrary")),
        name="conv_mixer",
    )(x, g.reshape(1, d), w_in, conv_w, mkv, mkv, w_o)


LOG2E = 1.4426950408889634


def _split_bf16(x):
    hi = x.astype(jnp.bfloat16)
    lo = (x - hi.astype(jnp.float32)).astype(jnp.bfloat16)
    return jnp.concatenate([hi, lo], axis=1)


SB_DONE_LOG2 = 160.0


def _sb_attention_kernel(q_ref, k_ref, v_ref, o_ref):
    tq = q_ref.shape[0]
    tk = tq
    heads = range(V7X_LANES // HEAD_DIM)
    qi = pl.program_id(2)
    q = q_ref[...]
    lane = lax.broadcasted_iota(jnp.int32, (1, V7X_LANES), 1)
    in_head = [(lane >= h * HEAD_DIM) & (lane < (h + 1) * HEAD_DIM) for h in heads]
    last_lane = lane == V7X_LANES - 1
    qh = [jnp.where(m, q, jnp.zeros_like(q)) * (HEAD_DIM ** -0.5) for m in in_head]
    r = lax.broadcasted_iota(jnp.int32, (tk, tk), 0)
    s = lax.broadcasted_iota(jnp.int32, (tk, tk), 1)
    suffix = jnp.where(r >= s, 1.0, 0.0).astype(jnp.bfloat16)
    suffix2 = jnp.concatenate([suffix, suffix], axis=0)
    strictly_before = r > s

    def sweep(blocks, c_last, acc):
        kb = [k_ref[pl.ds(pl.multiple_of(j * tk, tk), tk), :] for j, _ in blocks]
        vb = [v_ref[pl.ds(pl.multiple_of(j * tk, tk), tk), :] for j, _ in blocks]
        y = [[lax.dot_general(qh[h], kb[n], _NT, preferred_element_type=jnp.float32) * LOG2E
              for h in heads] for n in range(len(blocks))]
        p = [[jnp.maximum(y[n][h], 0.0) + jnp.log2(1.0 + jnp.exp2(-jnp.abs(y[n][h])))
              for h in heads] for n in range(len(blocks))]
        total = []
        for n, (_, diagonal) in enumerate(blocks):
            pn = p[n]
            if diagonal:
                pn = [jnp.where(strictly_before, pn[h], 0.0) for h in heads]
            if c_last is not None:
                pn = [jnp.concatenate([pn[h][:, :tk - V7X_LANES],
                                       pn[h][:, tk - V7X_LANES:] + c_last[h]], axis=1)
                      for h in heads]
            total.append([_dot(_split_bf16(pn[h]), suffix2) for h in heads])
            c = [total[n][h][:, 0:1] for h in heads]
            c_last = [jnp.where(last_lane, c[h], 0.0) for h in heads]
        for n, (_, diagonal) in enumerate(blocks):
            a = [jnp.exp2(y[n][h] - total[n][h]) for h in heads]
            if diagonal:
                a = [jnp.where(strictly_before, a[h], 0.0) for h in heads]
            acc = [acc[h] + _dot(a[h].astype(jnp.bfloat16), vb[n]) for h in heads]
        c_min = jnp.minimum(jnp.min(c[0]), jnp.min(c[1]))
        return c_min, c_last, acc

    def store(acc):
        o_ref[...] = jnp.where(in_head[0], acc[0], acc[1]).astype(o_ref.dtype)

    acc0 = [jnp.zeros((tq, V7X_LANES), jnp.float32) for _ in heads]

    @pl.when(qi == 0)
    def _():
        store(sweep([(qi, True)], None, acc0)[2])

    @pl.when(qi > 0)
    def _():
        state = (qi - 2,) + sweep([(qi, True), (qi - 1, False)], None, acc0)

        def unfinished(state):
            return (state[0] >= 0) & (state[1] < SB_DONE_LOG2)

        def body(state):
            j, _, c_last, acc = state
            return (j - 1,) + sweep([(j, False)], c_last, acc)

        store(lax.while_loop(unfinished, body, state)[3])


def _sb_attention(p, kv, batch, main):
    m = p.shape[0]
    seq = m // batch
    nq = seq // SB_TILE
    n_pairs = main // V7X_LANES
    return pl.pallas_call(
        _sb_attention_kernel,
        grid=(batch, n_pairs, nq),
        in_specs=[pl.BlockSpec((SB_TILE, V7X_LANES), lambda b, hp, qi: (b * nq + qi, hp)),
                  pl.BlockSpec((seq, V7X_LANES), lambda b, hp, qi: (b, hp)),
                  pl.BlockSpec((seq, V7X_LANES), lambda b, hp, qi: (b, n_pairs + hp))],
        out_specs=pl.BlockSpec((SB_TILE, V7X_LANES), lambda b, hp, qi: (b * nq + qi, hp)),
        out_shape=jax.ShapeDtypeStruct((m, main), jnp.bfloat16),
        compiler_params=_params(("parallel", "parallel", "parallel")),
        name="sb_attention",
    )(p, kv, kv)


def _attn_out_kernel(x_ref, ymain_ref, qmem_ref, mk_ref, mv_ref, wo_ref, o_ref):
    y_mem = _memory_attention(qmem_ref[...].astype(jnp.float32), mk_ref[...], mv_ref[...])
    o_ref[...] = _out_projection(x_ref[...], ymain_ref[...], y_mem, wo_ref)


def _attn_out(x, y_main, p, mkv, layer, w_o, batch):
    m, d = x.shape
    main = y_main.shape[1]
    tm = ROW_TILE
    nt = m // batch // tm
    n_mem = mkv.shape[0] // batch
    return pl.pallas_call(
        _attn_out_kernel,
        grid=(batch, nt),
        in_specs=[pl.BlockSpec((tm, d), lambda b, t: (b * nt + t, 0)),
                  pl.BlockSpec((tm, main), lambda b, t: (b * nt + t, 0)),
                  pl.BlockSpec((tm, MEM_WIDTH), lambda b, t: (b * nt + t, main // MEM_WIDTH)),
                  pl.BlockSpec((n_mem, MEM_WIDTH), lambda b, t: (b, 2 * layer)),
                  pl.BlockSpec((n_mem, MEM_WIDTH), lambda b, t: (b, 2 * layer + 1)),
                  _resident((d, d))],
        out_specs=pl.BlockSpec((tm, d), lambda b, t: (b * nt + t, 0)),
        out_shape=jax.ShapeDtypeStruct((m, d), jnp.float32),
        compiler_params=_params(("parallel", "parallel")),
        name="attn_out",
    )(x, y_main, p, mkv, mkv, w_o)


def _ffn_kernel(x_ref, g_ref, wg_ref, wu_ref, wd_ref, fg_ref, o_ref, *, final_norm):
    x = x_ref[...]
    h = _rmsnorm(x, g_ref[...]).astype(jnp.bfloat16)
    d_ff = wg_ref.shape[1]
    chunk = d_ff // FFN_CHUNKS
    y = x
    for c in range(FFN_CHUNKS):
        cols = slice(c * chunk, (c + 1) * chunk)
        gate = _dot(h, wg_ref[:, cols])
        up = _dot(h, wu_ref[:, cols])
        act = (gate / (1.0 + jnp.exp(-gate))) * up
        y = y + _dot(act.astype(jnp.bfloat16), wd_ref[cols, :])
    if final_norm:
        y = _rmsnorm(y, fg_ref[...])
    o_ref[...] = y


def _ffn(x, g, w_gate, w_up, w_down, final_g, final_norm):
    m, d = x.shape
    d_ff = w_gate.shape[1]
    assert d_ff % (FFN_CHUNKS * V7X_LANES) == 0
    tm = ROW_TILE
    return pl.pallas_call(
        partial(_ffn_kernel, final_norm=final_norm),
        grid=(m // tm,),
        in_specs=[pl.BlockSpec((tm, d), lambda i: (i, 0)),
                  _resident((1, d)),
                  _resident((d, d_ff)),
                  _resident((d, d_ff)),
                  _resident((d_ff, d)),
                  _resident((1, d))],
        out_specs=pl.BlockSpec((tm, d), lambda i: (i, 0)),
        out_shape=jax.ShapeDtypeStruct((m, d), jnp.float32),
        compiler_params=_params(("parallel",)),
        name="ffn",
    )(x, g.reshape(1, d), w_gate, w_up, w_down, final_g.reshape(1, d))


def kernel(x, mem, mix_norm, a_in, conv_w, b_in, kv_norm, w_kv_shared, w_mem_kv, w_o, ffn_norm,
           w_gate, w_up, w_down, mem_norm, final_norm):
    batch, seq, d = x.shape
    depth = mix_norm.shape[0]
    n_a = a_in.shape[0]
    main = conv_w.shape[2]
    bf16 = jnp.bfloat16
    assert seq % ROW_TILE == 0 and seq % SB_TILE == 0 and main % V7X_LANES == 0

    xs = x.reshape(batch * seq, d)
    w_mkv = jnp.transpose(w_mem_kv, (1, 0, 2)).reshape(d, depth * 2 * MEM_WIDTH).astype(bf16)
    mkv = _norm_matmul(mem.reshape(-1, d), mem_norm, w_mkv)

    kv = None
    for i in range(depth):
        wo_i = w_o[i].astype(bf16)
        if i < n_a:
            xs = _conv_mixer(xs, mix_norm[i], a_in[i].astype(bf16), conv_w[i], mkv, i, wo_i, batch)
        else:
            p = _norm_matmul(xs, mix_norm[i], b_in[i - n_a].astype(bf16))
            y_main = _sb_attention(p, kv, batch, main)
            xs = _attn_out(xs, y_main, p, mkv, i, wo_i, batch)
        last = i == depth - 1
        xs = _ffn(xs, ffn_norm[i], w_gate[i].astype(bf16), w_up[i].astype(bf16),
                  w_down[i].astype(bf16), final_norm, last)
        if i == n_a - 1:
            kv = _norm_matmul(xs, kv_norm, w_kv_shared.astype(bf16))
    return xs.reshape(batch, seq, d)
```

```python
from functools import partial

import jax
import jax.numpy as jnp
from jax import lax
from jax.experimental import pallas as pl
from jax.experimental.pallas import tpu as pltpu

HEAD_DIM = 64
MEM_HEADS = 4
MEM_WIDTH = MEM_HEADS * HEAD_DIM
CONV_WIDTH = 3
EPS = 1e-6

V7X_LANES = 128
V7X_SUBLANES = 8
V7X_MXU_WIDTH = 256
V7X_VMEM_LIMIT_BYTES = 56 * 1024 * 1024

ROW_TILE = 1024
SB_TILE = 256
SB_Q_TILES = 2
FFN_CHUNKS = 2

_NT = (((1,), (1,)), ((), ()))


def _rmsnorm(x, g):
    y = x * lax.rsqrt(jnp.mean(x * x, axis=-1, keepdims=True) + EPS)
    return y * g


def _dot(a, b):
    return jnp.dot(a, b, preferred_element_type=jnp.float32)


def _resident(shape):
    return pl.BlockSpec(shape, lambda *_: (0,) * len(shape), pipeline_mode=pl.Buffered(1))


def _params(semantics):
    return pltpu.CompilerParams(dimension_semantics=semantics,
                                vmem_limit_bytes=V7X_VMEM_LIMIT_BYTES)


def _norm_matmul_kernel(x_ref, g_ref, w_ref, o_ref):
    h = _rmsnorm(x_ref[...], g_ref[...]).astype(jnp.bfloat16)
    o_ref[...] = _dot(h, w_ref[...]).astype(o_ref.dtype)


def _norm_matmul(x, g, w):
    m, d = x.shape
    n = w.shape[1]
    tm = min(ROW_TILE, m)
    return pl.pallas_call(
        _norm_matmul_kernel,
        grid=(m // tm,),
        in_specs=[pl.BlockSpec((tm, d), lambda i: (i, 0)),
                  _resident((1, d)),
                  _resident((d, n))],
        out_specs=pl.BlockSpec((tm, n), lambda i: (i, 0)),
        out_shape=jax.ShapeDtypeStruct((m, n), jnp.bfloat16),
        compiler_params=_params(("parallel",)),
        name="norm_matmul",
    )(x, g.reshape(1, d), w)


def _memory_attention(q_mem, mk, mv):
    lane = lax.broadcasted_iota(jnp.int32, (1, MEM_WIDTH), 1)
    scale = 1.0 / (HEAD_DIM ** 0.5)
    out = jnp.zeros(q_mem.shape, jnp.float32)
    for h in range(MEM_HEADS):
        in_head = (lane >= h * HEAD_DIM) & (lane < (h + 1) * HEAD_DIM)
        qh = jnp.where(in_head, q_mem, 0.0).astype(jnp.bfloat16)
        s = lax.dot_general(qh, mk, _NT, preferred_element_type=jnp.float32) * scale
        p = jnp.exp(s - jnp.max(s, axis=-1, keepdims=True))
        l = jnp.sum(p, axis=-1, keepdims=True)
        vh = jnp.where(in_head, mv, jnp.zeros_like(mv))
        out = out + _dot(p.astype(jnp.bfloat16), vh) / l
    return out


def _out_projection(x, y_main, y_mem, wo_ref):
    main_width = y_main.shape[1]
    y = _dot(y_main.astype(jnp.bfloat16), wo_ref[:main_width, :])
    y = y + _dot(y_mem.astype(jnp.bfloat16), wo_ref[main_width:, :])
    return x + y


def _conv_mixer_kernel(x_ref, g_ref, win_ref, cw_ref, mk_ref, mv_ref, wo_ref, o_ref, tail_ref):
    main = cw_ref.shape[1]
    x = x_ref[...]
    h = _rmsnorm(x, g_ref[...]).astype(jnp.bfloat16)
    p = _dot(h, win_ref[...])
    b_gate = p[:, :main]
    cu = p[:, main:2 * main] * p[:, 2 * main:3 * main]
    q_mem = p[:, 3 * main:]

    @pl.when(pl.program_id(1) == 0)
    def _():
        tail_ref[...] = jnp.zeros_like(tail_ref)

    prev = tail_ref[...]
    rows = lax.broadcasted_iota(jnp.int32, prev.shape, 0)
    conv = cu * cw_ref[CONV_WIDTH - 1:CONV_WIDTH, :]
    for back in range(1, CONV_WIDTH):
        shifted = pltpu.roll(cu, back, 0)
        top = jnp.where(rows < back, pltpu.roll(prev, back, 0), shifted[:V7X_SUBLANES])
        shifted = jnp.concatenate([top, shifted[V7X_SUBLANES:]], axis=0)
        conv = conv + shifted * cw_ref[CONV_WIDTH - 1 - back:CONV_WIDTH - back, :]
    tail_ref[...] = cu[cu.shape[0] - V7X_SUBLANES:]

    y_main = b_gate * conv
    y_mem = _memory_attention(q_mem, mk_ref[...], mv_ref[...])
    o_ref[...] = _out_projection(x, y_main, y_mem, wo_ref)


def _conv_mixer(x, g, w_in, conv_w, mkv, layer, w_o, batch):
    m, d = x.shape
    n_in = w_in.shape[1]
    main = conv_w.shape[1]
    seq = m // batch
    tm = ROW_TILE
    nt = seq // tm
    n_mem = mkv.shape[0] // batch
    return pl.pallas_call(
        _conv_mixer_kernel,
        grid=(batch, nt),
        in_specs=[pl.BlockSpec((tm, d), lambda b, t: (b * nt + t, 0)),
                  _resident((1, d)),
                  _resident((d, n_in)),
                  _resident((CONV_WIDTH, main)),
                  pl.BlockSpec((n_mem, MEM_WIDTH), lambda b, t: (b, 2 * layer)),
                  pl.BlockSpec((n_mem, MEM_WIDTH), lambda b, t: (b, 2 * layer + 1)),
                  _resident((d, d))],
        out_specs=pl.BlockSpec((tm, d), lambda b, t: (b * nt + t, 0)),
        out_shape=jax.ShapeDtypeStruct((m, d), jnp.float32),
        scratch_shapes=[pltpu.VMEM((V7X_SUBLANES, main), jnp.float32)],
        compiler_params=_params(("arbitrary", "arbitrary")),
        name="conv_mixer",
    )(x, g.reshape(1, d), w_in, conv_w, mkv, mkv, w_o)


LOG2E = 1.4426950408889634


def _split_bf16(x):
    hi = x.astype(jnp.bfloat16)
    lo = (x - hi.astype(jnp.float32)).astype(jnp.bfloat16)
    return jnp.concatenate([hi, lo], axis=1)


SB_DONE_LOG2 = 160.0


def _sb_attention_kernel(q_ref, k_ref, v_ref, o_ref):
    tk = SB_TILE
    heads = range(V7X_LANES // HEAD_DIM)
    n_sub = q_ref.shape[0] // tk
    first_tile = pl.program_id(2) * n_sub
    lane = lax.broadcasted_iota(jnp.int32, (1, V7X_LANES), 1)
    in_head = [(lane >= h * HEAD_DIM) & (lane < (h + 1) * HEAD_DIM) for h in heads]
    last_lane = lane == V7X_LANES - 1
    r = lax.broadcasted_iota(jnp.int32, (tk, tk), 0)
    s = lax.broadcasted_iota(jnp.int32, (tk, tk), 1)
    suffix = jnp.where(r >= s, 1.0, 0.0).astype(jnp.bfloat16)
    suffix2 = jnp.concatenate([suffix, suffix], axis=0)
    strictly_before = r > s

    def queries(u):
        q = q_ref[u * tk:(u + 1) * tk, :]
        return [jnp.where(m, q, jnp.zeros_like(q)) * (HEAD_DIM ** -0.5) for m in in_head]

    def sweep(jobs):
        chains = [(u, n, h) for u, job in enumerate(jobs) for n in range(len(job[1])) for h in heads]
        kv = {}
        for u, job in enumerate(jobs):
            for n, (j, _) in enumerate(job[1]):
                rows = pl.ds(pl.multiple_of(j * tk, tk), tk)
                kv[u, n] = (k_ref[rows, :], v_ref[rows, :])
        y = {(u, n, h): lax.dot_general(jobs[u][0][h], kv[u, n][0], _NT,
                                        preferred_element_type=jnp.float32) * LOG2E
             for u, n, h in chains}
        p = {c: jnp.maximum(y[c], 0.0) + jnp.log2(1.0 + jnp.exp2(-jnp.abs(y[c]))) for c in chains}
        total = {}
        c_last = [job[2] for job in jobs]
        for n in range(max(len(job[1]) for job in jobs)):
            for u, job in enumerate(jobs):
                if n >= len(job[1]):
                    continue
                for h in heads:
                    pn = p[u, n, h]
                    if job[1][n][1]:
                        pn = jnp.where(strictly_before, pn, 0.0)
                    if c_last[u] is not None:
                        pn = jnp.concatenate([pn[:, :tk - V7X_LANES],
                                              pn[:, tk - V7X_LANES:] + c_last[u][h]], axis=1)
                    total[u, n, h] = _dot(_split_bf16(pn), suffix2)
                c_last[u] = [jnp.where(last_lane, total[u, n, h][:, 0:1], 0.0) for h in heads]
        acc = [list(job[3]) for job in jobs]
        for u, n, h in chains:
            a = jnp.exp2(y[u, n, h] - total[u, n, h])
            if jobs[u][1][n][1]:
                a = jnp.where(strictly_before, a, 0.0)
            acc[u][h] = acc[u][h] + _dot(a.astype(jnp.bfloat16), kv[u, n][1])
        out = []
        for u, job in enumerate(jobs):
            last = len(job[1]) - 1
            c_min = jnp.minimum(jnp.min(total[u, last, 0][:, 0:1]), jnp.min(total[u, last, 1][:, 0:1]))
            out.append((c_min, c_last[u], acc[u]))
        return out

    def store(u, acc):
        o_ref[u * tk:(u + 1) * tk, :] = jnp.where(in_head[0], acc[0], acc[1]).astype(o_ref.dtype)

    def finish(u, qh, tile, state):
        def unfinished(state):
            return (state[0] >= 0) & (state[1] < SB_DONE_LOG2)

        def body(state):
            j, _, c_last, acc = state
            return (j - 1,) + sweep([(qh, [(j, False)], c_last, acc)])[0]

        store(u, lax.while_loop(unfinished, body, (tile - 2,) + state)[3])

    zeros = [jnp.zeros((tk, V7X_LANES), jnp.float32) for _ in heads]
    qs = [queries(u) for u in range(n_sub)]

    @pl.when(first_tile == 0)
    def _():
        jobs = [(qs[u], [(u, True)] + ([(u - 1, False)] if u > 0 else []), None, zeros)
                for u in range(n_sub)]
        for u, state in enumerate(sweep(jobs)):
            if u < 2:
                store(u, state[2])
            else:
                finish(u, qs[u], u, state)

    @pl.when(first_tile > 0)
    def _():
        jobs = [(qs[u], [(first_tile + u, True), (first_tile + u - 1, False)], None, zeros)
                for u in range(n_sub)]
        for u, state in enumerate(sweep(jobs)):
            finish(u, qs[u], first_tile + u, state)


def _sb_attention(p, kv, batch, main):
    m = p.shape[0]
    seq = m // batch
    tq = SB_TILE * SB_Q_TILES
    nq = seq // tq
    n_pairs = main // V7X_LANES
    return pl.pallas_call(
        _sb_attention_kernel,
        grid=(batch, n_pairs, nq),
        in_specs=[pl.BlockSpec((tq, V7X_LANES), lambda b, hp, qi: (b * nq + qi, hp)),
                  pl.BlockSpec((seq, V7X_LANES), lambda b, hp, qi: (b, hp)),
                  pl.BlockSpec((seq, V7X_LANES), lambda b, hp, qi: (b, n_pairs + hp))],
        out_specs=pl.BlockSpec((tq, V7X_LANES), lambda b, hp, qi: (b * nq + qi, hp)),
        out_shape=jax.ShapeDtypeStruct((m, main), jnp.bfloat16),
        compiler_params=_params(("parallel", "parallel", "parallel")),
        name="sb_attention",
    )(p, kv, kv)


def _attn_out_kernel(x_ref, ymain_ref, qmem_ref, mk_ref, mv_ref, wo_ref, o_ref):
    y_mem = _memory_attention(qmem_ref[...].astype(jnp.float32), mk_ref[...], mv_ref[...])
    o_ref[...] = _out_projection(x_ref[...], ymain_ref[...], y_mem, wo_ref)


def _attn_out(x, y_main, p, mkv, layer, w_o, batch):
    m, d = x.shape
    main = y_main.shape[1]
    tm = ROW_TILE
    nt = m // batch // tm
    n_mem = mkv.shape[0] // batch
    return pl.pallas_call(
        _attn_out_kernel,
        grid=(batch, nt),
        in_specs=[pl.BlockSpec((tm, d), lambda b, t: (b * nt + t, 0)),
                  pl.BlockSpec((tm, main), lambda b, t: (b * nt + t, 0)),
                  pl.BlockSpec((tm, MEM_WIDTH), lambda b, t: (b * nt + t, main // MEM_WIDTH)),
                  pl.BlockSpec((n_mem, MEM_WIDTH), lambda b, t: (b, 2 * layer)),
                  pl.BlockSpec((n_mem, MEM_WIDTH), lambda b, t: (b, 2 * layer + 1)),
                  _resident((d, d))],
        out_specs=pl.BlockSpec((tm, d), lambda b, t: (b * nt + t, 0)),
        out_shape=jax.ShapeDtypeStruct((m, d), jnp.float32),
        compiler_params=_params(("parallel", "parallel")),
        name="attn_out",
    )(x, y_main, p, mkv, mkv, w_o)


def _ffn_kernel(x_ref, g_ref, wg_ref, wu_ref, wd_ref, fg_ref, o_ref, *, final_norm):
    x = x_ref[...]
    h = _rmsnorm(x, g_ref[...]).astype(jnp.bfloat16)
    d_ff = wg_ref.shape[1]
    chunk = pl.cdiv(d_ff // V7X_MXU_WIDTH, FFN_CHUNKS) * V7X_MXU_WIDTH
    y = x
    for start in range(0, d_ff, chunk):
        cols = slice(start, min(start + chunk, d_ff))
        gate = _dot(h, wg_ref[:, cols])
        up = _dot(h, wu_ref[:, cols])
        act = (gate / (1.0 + jnp.exp(-gate))) * up
        y = y + _dot(act.astype(jnp.bfloat16), wd_ref[cols, :])
    if final_norm:
        y = _rmsnorm(y, fg_ref[...])
    o_ref[...] = y


def _ffn(x, g, w_gate, w_up, w_down, final_g, final_norm):
    m, d = x.shape
    d_ff = w_gate.shape[1]
    assert d_ff % V7X_MXU_WIDTH == 0
    tm = ROW_TILE
    return pl.pallas_call(
        partial(_ffn_kernel, final_norm=final_norm),
        grid=(m // tm,),
        in_specs=[pl.BlockSpec((tm, d), lambda i: (i, 0)),
                  _resident((1, d)),
                  _resident((d, d_ff)),
                  _resident((d, d_ff)),
                  _resident((d_ff, d)),
                  _resident((1, d))],
        out_specs=pl.BlockSpec((tm, d), lambda i: (i, 0)),
        out_shape=jax.ShapeDtypeStruct((m, d), jnp.float32),
        compiler_params=_params(("parallel",)),
        name="ffn",
    )(x, g.reshape(1, d), w_gate, w_up, w_down, final_g.reshape(1, d))


def kernel(x, mem, mix_norm, a_in, conv_w, b_in, kv_norm, w_kv_shared, w_mem_kv, w_o, ffn_norm,
           w_gate, w_up, w_down, mem_norm, final_norm):
    batch, seq, d = x.shape
    depth = mix_norm.shape[0]
    n_a = a_in.shape[0]
    main = conv_w.shape[2]
    bf16 = jnp.bfloat16
    assert seq % ROW_TILE == 0 and seq % (SB_TILE * SB_Q_TILES) == 0 and main % V7X_LANES == 0

    xs = x.reshape(batch * seq, d)
    w_mkv = jnp.transpose(w_mem_kv, (1, 0, 2)).reshape(d, depth * 2 * MEM_WIDTH).astype(bf16)
    mkv = _norm_matmul(mem.reshape(-1, d), mem_norm, w_mkv)

    kv = None
    for i in range(depth):
        wo_i = w_o[i].astype(bf16)
        if i < n_a:
            xs = _conv_mixer(xs, mix_norm[i], a_in[i].astype(bf16), conv_w[i], mkv, i, wo_i, batch)
        else:
            p = _norm_matmul(xs, mix_norm[i], b_in[i - n_a].astype(bf16))
            y_main = _sb_attention(p, kv, batch, main)
            xs = _attn_out(xs, y_main, p, mkv, i, wo_i, batch)
        last = i == depth - 1
        xs = _ffn(xs, ffn_norm[i], w_gate[i].astype(bf16), w_up[i].astype(bf16),
                  w_down[i].astype(bf16), final_norm, last)
        if i == n_a - 1:
            kv = _norm_matmul(xs, kv_norm, w_kv_shared.astype(bf16))
    return xs.reshape(batch, seq, d)
```

```python
from functools import partial

import jax
import jax.numpy as jnp
from jax import lax
from jax.experimental import pallas as pl
from jax.experimental.pallas import tpu as pltpu

HEAD_DIM = 64
MEM_HEADS = 4
MEM_WIDTH = MEM_HEADS * HEAD_DIM
CONV_WIDTH = 3
EPS = 1e-6

V7X_LANES = 128
V7X_SUBLANES = 8
V7X_MXU_WIDTH = 256
V7X_VMEM_LIMIT_BYTES = 56 * 1024 * 1024

ROW_TILE = 1024
SB_TILE = 128
SB_Q_TILES = 4
SB_FUSED_BLOCKS = 3
FFN_CHUNKS = 2

_NT = (((1,), (1,)), ((), ()))


def _rmsnorm(x, g):
    y = x * lax.rsqrt(jnp.mean(x * x, axis=-1, keepdims=True) + EPS)
    return y * g


def _dot(a, b):
    return jnp.dot(a, b, preferred_element_type=jnp.float32)


def _resident(shape):
    return pl.BlockSpec(shape, lambda *_: (0,) * len(shape), pipeline_mode=pl.Buffered(1))


def _params(semantics):
    return pltpu.CompilerParams(dimension_semantics=semantics,
                                vmem_limit_bytes=V7X_VMEM_LIMIT_BYTES)


def _norm_matmul_kernel(x_ref, g_ref, w_ref, o_ref):
    h = _rmsnorm(x_ref[...], g_ref[...]).astype(jnp.bfloat16)
    o_ref[...] = _dot(h, w_ref[...]).astype(o_ref.dtype)


def _norm_matmul(x, g, w):
    m, d = x.shape
    n = w.shape[1]
    tm = min(ROW_TILE, m)
    return pl.pallas_call(
        _norm_matmul_kernel,
        grid=(m // tm,),
        in_specs=[pl.BlockSpec((tm, d), lambda i: (i, 0)),
                  _resident((1, d)),
                  _resident((d, n))],
        out_specs=pl.BlockSpec((tm, n), lambda i: (i, 0)),
        out_shape=jax.ShapeDtypeStruct((m, n), jnp.bfloat16),
        compiler_params=_params(("parallel",)),
        name="norm_matmul",
    )(x, g.reshape(1, d), w)


def _memory_attention(q_mem, mk, mv):
    lane = lax.broadcasted_iota(jnp.int32, (1, MEM_WIDTH), 1)
    scale = 1.0 / (HEAD_DIM ** 0.5)
    out = jnp.zeros(q_mem.shape, jnp.float32)
    for h in range(MEM_HEADS):
        in_head = (lane >= h * HEAD_DIM) & (lane < (h + 1) * HEAD_DIM)
        qh = jnp.where(in_head, q_mem, 0.0).astype(jnp.bfloat16)
        s = lax.dot_general(qh, mk, _NT, preferred_element_type=jnp.float32) * scale
        p = jnp.exp(s - jnp.max(s, axis=-1, keepdims=True))
        l = jnp.sum(p, axis=-1, keepdims=True)
        vh = jnp.where(in_head, mv, jnp.zeros_like(mv))
        out = out + _dot(p.astype(jnp.bfloat16), vh) / l
    return out


def _out_projection(x, y_main, y_mem, wo_ref):
    main_width = y_main.shape[1]
    y = _dot(y_main.astype(jnp.bfloat16), wo_ref[:main_width, :])
    y = y + _dot(y_mem.astype(jnp.bfloat16), wo_ref[main_width:, :])
    return x + y


def _conv_mixer_kernel(x_ref, g_ref, win_ref, cw_ref, mk_ref, mv_ref, wo_ref, o_ref, tail_ref):
    main = cw_ref.shape[1]
    x = x_ref[...]
    h = _rmsnorm(x, g_ref[...]).astype(jnp.bfloat16)
    p = _dot(h, win_ref[...])
    b_gate = p[:, :main]
    cu = p[:, main:2 * main] * p[:, 2 * main:3 * main]
    q_mem = p[:, 3 * main:]

    @pl.when(pl.program_id(1) == 0)
    def _():
        tail_ref[...] = jnp.zeros_like(tail_ref)

    prev = tail_ref[...]
    rows = lax.broadcasted_iota(jnp.int32, prev.shape, 0)
    conv = cu * cw_ref[CONV_WIDTH - 1:CONV_WIDTH, :]
    for back in range(1, CONV_WIDTH):
        shifted = pltpu.roll(cu, back, 0)
        top = jnp.where(rows < back, pltpu.roll(prev, back, 0), shifted[:V7X_SUBLANES])
        shifted = jnp.concatenate([top, shifted[V7X_SUBLANES:]], axis=0)
        conv = conv + shifted * cw_ref[CONV_WIDTH - 1 - back:CONV_WIDTH - back, :]
    tail_ref[...] = cu[cu.shape[0] - V7X_SUBLANES:]

    y_main = b_gate * conv
    y_mem = _memory_attention(q_mem, mk_ref[...], mv_ref[...])
    o_ref[...] = _out_projection(x, y_main, y_mem, wo_ref)


def _conv_mixer(x, g, w_in, conv_w, mkv, layer, w_o, batch):
    m, d = x.shape
    n_in = w_in.shape[1]
    main = conv_w.shape[1]
    seq = m // batch
    tm = ROW_TILE
    nt = seq // tm
    n_mem = mkv.shape[0] // batch
    return pl.pallas_call(
        _conv_mixer_kernel,
        grid=(batch, nt),
        in_specs=[pl.BlockSpec((tm, d), lambda b, t: (b * nt + t, 0)),
                  _resident((1, d)),
                  _resident((d, n_in)),
                  _resident((CONV_WIDTH, main)),
                  pl.BlockSpec((n_mem, MEM_WIDTH), lambda b, t: (b, 2 * layer)),
                  pl.BlockSpec((n_mem, MEM_WIDTH), lambda b, t: (b, 2 * layer + 1)),
                  _resident((d, d))],
        out_specs=pl.BlockSpec((tm, d), lambda b, t: (b * nt + t, 0)),
        out_shape=jax.ShapeDtypeStruct((m, d), jnp.float32),
        scratch_shapes=[pltpu.VMEM((V7X_SUBLANES, main), jnp.float32)],
        compiler_params=_params(("arbitrary", "arbitrary")),
        name="conv_mixer",
    )(x, g.reshape(1, d), w_in, conv_w, mkv, mkv, w_o)


LOG2E = 1.4426950408889634


def _split_bf16(x):
    hi = x.astype(jnp.bfloat16)
    lo = (x - hi.astype(jnp.float32)).astype(jnp.bfloat16)
    return jnp.concatenate([hi, lo], axis=1)


SB_DONE_LOG2 = 160.0


def _sb_attention_kernel(q_ref, k_ref, v_ref, o_ref):
    tk = SB_TILE
    heads = range(V7X_LANES // HEAD_DIM)
    n_sub = q_ref.shape[0] // tk
    first_tile = pl.program_id(2) * n_sub
    lane = lax.broadcasted_iota(jnp.int32, (1, V7X_LANES), 1)
    in_head = [(lane >= h * HEAD_DIM) & (lane < (h + 1) * HEAD_DIM) for h in heads]
    last_lane = lane == V7X_LANES - 1
    r = lax.broadcasted_iota(jnp.int32, (tk, tk), 0)
    s = lax.broadcasted_iota(jnp.int32, (tk, tk), 1)
    suffix = jnp.where(r >= s, 1.0, 0.0).astype(jnp.bfloat16)
    suffix2 = jnp.concatenate([suffix, suffix], axis=0)
    strictly_before = r > s

    def queries(u):
        q = q_ref[u * tk:(u + 1) * tk, :]
        return [jnp.where(m, q, jnp.zeros_like(q)) * (HEAD_DIM ** -0.5) for m in in_head]

    def sweep(jobs):
        chains = [(u, n, h) for u, job in enumerate(jobs) for n in range(len(job[1])) for h in heads]
        kv = {}
        for u, job in enumerate(jobs):
            for n, (j, _) in enumerate(job[1]):
                rows = pl.ds(pl.multiple_of(j * tk, tk), tk)
                kv[u, n] = (k_ref[rows, :], v_ref[rows, :])
        y = {(u, n, h): lax.dot_general(jobs[u][0][h], kv[u, n][0], _NT,
                                        preferred_element_type=jnp.float32) * LOG2E
             for u, n, h in chains}
        p = {c: jnp.maximum(y[c], 0.0) + jnp.log2(1.0 + jnp.exp2(-jnp.abs(y[c]))) for c in chains}
        total = {}
        c_last = [job[2] for job in jobs]
        for n in range(max(len(job[1]) for job in jobs)):
            for u, job in enumerate(jobs):
                if n >= len(job[1]):
                    continue
                for h in heads:
                    pn = p[u, n, h]
                    if job[1][n][1]:
                        pn = jnp.where(strictly_before, pn, 0.0)
                    if c_last[u] is not None:
                        tail = pn[:, tk - V7X_LANES:] + c_last[u][h]
                        pn = tail if tk == V7X_LANES else jnp.concatenate(
                            [pn[:, :tk - V7X_LANES], tail], axis=1)
                    total[u, n, h] = _dot(_split_bf16(pn), suffix2)
                c_last[u] = [jnp.where(last_lane, total[u, n, h][:, 0:1], 0.0) for h in heads]
        acc = [list(job[3]) for job in jobs]
        for u, n, h in chains:
            a = jnp.exp2(y[u, n, h] - total[u, n, h])
            if jobs[u][1][n][1]:
                a = jnp.where(strictly_before, a, 0.0)
            acc[u][h] = acc[u][h] + _dot(a.astype(jnp.bfloat16), kv[u, n][1])
        out = []
        for u, job in enumerate(jobs):
            last = len(job[1]) - 1
            c_min = jnp.minimum(jnp.min(total[u, last, 0][:, 0:1]), jnp.min(total[u, last, 1][:, 0:1]))
            out.append((c_min, c_last[u], acc[u]))
        return out

    def store(u, acc):
        o_ref[u * tk:(u + 1) * tk, :] = jnp.where(in_head[0], acc[0], acc[1]).astype(o_ref.dtype)

    def finish(u, qh, next_block, state):
        def unfinished(state):
            return (state[0] >= 0) & (state[1] < SB_DONE_LOG2)

        def body(state):
            j, _, c_last, acc = state
            return (j - 1,) + sweep([(qh, [(j, False)], c_last, acc)])[0]

        store(u, lax.while_loop(unfinished, body, (next_block,) + state)[3])

    zeros = [jnp.zeros((tk, V7X_LANES), jnp.float32) for _ in heads]
    qs = [queries(u) for u in range(n_sub)]

    def first_blocks(tile, count):
        return [(tile, True)] + [(tile - i, False) for i in range(1, count)]

    @pl.when(first_tile == 0)
    def _():
        jobs = [(qs[u], first_blocks(u, min(u + 1, SB_FUSED_BLOCKS)), None, zeros)
                for u in range(n_sub)]
        for u, state in enumerate(sweep(jobs)):
            if u < SB_FUSED_BLOCKS:
                store(u, state[2])
            else:
                finish(u, qs[u], u - SB_FUSED_BLOCKS, state)

    @pl.when(first_tile > 0)
    def _():
        jobs = [(qs[u], first_blocks(first_tile + u, SB_FUSED_BLOCKS), None, zeros)
                for u in range(n_sub)]
        for u, state in enumerate(sweep(jobs)):
            finish(u, qs[u], first_tile + u - SB_FUSED_BLOCKS, state)


def _sb_attention(p, kv, batch, main):
    m = p.shape[0]
    seq = m // batch
    tq = SB_TILE * SB_Q_TILES
    nq = seq // tq
    n_pairs = main // V7X_LANES
    return pl.pallas_call(
        _sb_attention_kernel,
        grid=(batch, n_pairs, nq),
        in_specs=[pl.BlockSpec((tq, V7X_LANES), lambda b, hp, qi: (b * nq + qi, hp)),
                  pl.BlockSpec((seq, V7X_LANES), lambda b, hp, qi: (b, hp)),
                  pl.BlockSpec((seq, V7X_LANES), lambda b, hp, qi: (b, n_pairs + hp))],
        out_specs=pl.BlockSpec((tq, V7X_LANES), lambda b, hp, qi: (b * nq + qi, hp)),
        out_shape=jax.ShapeDtypeStruct((m, main), jnp.bfloat16),
        compiler_params=_params(("parallel", "parallel", "parallel")),
        name="sb_attention",
    )(p, kv, kv)


def _attn_out_kernel(x_ref, ymain_ref, qmem_ref, mk_ref, mv_ref, wo_ref, o_ref):
    y_mem = _memory_attention(qmem_ref[...].astype(jnp.float32), mk_ref[...], mv_ref[...])
    o_ref[...] = _out_projection(x_ref[...], ymain_ref[...], y_mem, wo_ref)


def _attn_out(x, y_main, p, mkv, layer, w_o, batch):
    m, d = x.shape
    main = y_main.shape[1]
    tm = ROW_TILE
    nt = m // batch // tm
    n_mem = mkv.shape[0] // batch
    return pl.pallas_call(
        _attn_out_kernel,
        grid=(batch, nt),
        in_specs=[pl.BlockSpec((tm, d), lambda b, t: (b * nt + t, 0)),
                  pl.BlockSpec((tm, main), lambda b, t: (b * nt + t, 0)),
                  pl.BlockSpec((tm, MEM_WIDTH), lambda b, t: (b * nt + t, main // MEM_WIDTH)),
                  pl.BlockSpec((n_mem, MEM_WIDTH), lambda b, t: (b, 2 * layer)),
                  pl.BlockSpec((n_mem, MEM_WIDTH), lambda b, t: (b, 2 * layer + 1)),
                  _resident((d, d))],
        out_specs=pl.BlockSpec((tm, d), lambda b, t: (b * nt + t, 0)),
        out_shape=jax.ShapeDtypeStruct((m, d), jnp.float32),
        compiler_params=_params(("parallel", "parallel")),
        name="attn_out",
    )(x, y_main, p, mkv, mkv, w_o)


def _ffn_kernel(x_ref, g_ref, wg_ref, wu_ref, wd_ref, fg_ref, o_ref, *, final_norm):
    x = x_ref[...]
    h = _rmsnorm(x, g_ref[...]).astype(jnp.bfloat16)
    d_ff = wg_ref.shape[1]
    chunk = pl.cdiv(d_ff // V7X_MXU_WIDTH, FFN_CHUNKS) * V7X_MXU_WIDTH
    y = x
    for start in range(0, d_ff, chunk):
        cols = slice(start, min(start + chunk, d_ff))
        gate = _dot(h, wg_ref[:, cols])
        up = _dot(h, wu_ref[:, cols])
        act = (gate / (1.0 + jnp.exp(-gate))) * up
        y = y + _dot(act.astype(jnp.bfloat16), wd_ref[cols, :])
    if final_norm:
        y = _rmsnorm(y, fg_ref[...])
    o_ref[...] = y


def _ffn(x, g, w_gate, w_up, w_down, final_g, final_norm):
    m, d = x.shape
    d_ff = w_gate.shape[1]
    assert d_ff % V7X_MXU_WIDTH == 0
    tm = ROW_TILE
    return pl.pallas_call(
        partial(_ffn_kernel, final_norm=final_norm),
        grid=(m // tm,),
        in_specs=[pl.BlockSpec((tm, d), lambda i: (i, 0)),
                  _resident((1, d)),
                  _resident((d, d_ff)),
                  _resident((d, d_ff)),
                  _resident((d_ff, d)),
                  _resident((1, d))],
        out_specs=pl.BlockSpec((tm, d), lambda i: (i, 0)),
        out_shape=jax.ShapeDtypeStruct((m, d), jnp.float32),
        compiler_params=_params(("parallel",)),
        name="ffn",
    )(x, g.reshape(1, d), w_gate, w_up, w_down, final_g.reshape(1, d))


def kernel(x, mem, mix_norm, a_in, conv_w, b_in, kv_norm, w_kv_shared, w_mem_kv, w_o, ffn_norm,
           w_gate, w_up, w_down, mem_norm, final_norm):
    batch, seq, d = x.shape
    depth = mix_norm.shape[0]
    n_a = a_in.shape[0]
    main = conv_w.shape[2]
    bf16 = jnp.bfloat16
    assert seq % ROW_TILE == 0 and seq % (SB_TILE * SB_Q_TILES) == 0 and main % V7X_LANES == 0

    xs = x.reshape(batch * seq, d)
    w_mkv = jnp.transpose(w_mem_kv, (1, 0, 2)).reshape(d, depth * 2 * MEM_WIDTH).astype(bf16)
    mkv = _norm_matmul(mem.reshape(-1, d), mem_norm, w_mkv)

    kv = None
    for i in range(depth):
        wo_i = w_o[i].astype(bf16)
        if i < n_a:
            xs = _conv_mixer(xs, mix_norm[i], a_in[i].astype(bf16), conv_w[i], mkv, i, wo_i, batch)
        else:
            p = _norm_matmul(xs, mix_norm[i], b_in[i - n_a].astype(bf16))
            y_main = _sb_attention(p, kv, batch, main)
            xs = _attn_out(xs, y_main, p, mkv, i, wo_i, batch)
        last = i == depth - 1
        xs = _ffn(xs, ffn_norm[i], w_gate[i].astype(bf16), w_up[i].astype(bf16),
                  w_down[i].astype(bf16), final_norm, last)
        if i == n_a - 1:
            kv = _norm_matmul(xs, kv_norm, w_kv_shared.astype(bf16))
    return xs.reshape(batch, seq, d)
```

```python
from functools import partial

import jax
import jax.numpy as jnp
from jax import lax
from jax.experimental import pallas as pl
from jax.experimental.pallas import tpu as pltpu

HEAD_DIM = 64
MEM_HEADS = 4
MEM_WIDTH = MEM_HEADS * HEAD_DIM
CONV_WIDTH = 3
EPS = 1e-6

V7X_LANES = 128
V7X_SUBLANES = 8
V7X_MXU_WIDTH = 256
V7X_VMEM_LIMIT_BYTES = 56 * 1024 * 1024

ROW_TILE = 1024
SB_TILE = 128
SB_Q_TILES = 4
SB_FUSED_BLOCKS = 3
FFN_CHUNKS = 2

_NT = (((1,), (1,)), ((), ()))


def _rmsnorm(x, g):
    y = x * lax.rsqrt(jnp.mean(x * x, axis=-1, keepdims=True) + EPS)
    return y * g


def _dot(a, b):
    return jnp.dot(a, b, preferred_element_type=jnp.float32)


def _layer_spec(param):
    stacked, layer = param
    shape = stacked.shape[1:]
    return pl.BlockSpec((None,) + shape, lambda *_: (layer,) + (0,) * len(shape),
                        pipeline_mode=pl.Buffered(1))


def _params(semantics):
    return pltpu.CompilerParams(dimension_semantics=semantics,
                                vmem_limit_bytes=V7X_VMEM_LIMIT_BYTES)


def _norm_matmul_kernel(x_ref, g_ref, w_ref, o_ref):
    h = _rmsnorm(x_ref[...], g_ref[...]).astype(jnp.bfloat16)
    o_ref[...] = _dot(h, w_ref[...]).astype(o_ref.dtype)


def _norm_matmul(x, g, w):
    m, d = x.shape
    n = w[0].shape[2]
    tm = min(ROW_TILE, m)
    return pl.pallas_call(
        _norm_matmul_kernel,
        grid=(m // tm,),
        in_specs=[pl.BlockSpec((tm, d), lambda i: (i, 0)),
                  _layer_spec(g),
                  _layer_spec(w)],
        out_specs=pl.BlockSpec((tm, n), lambda i: (i, 0)),
        out_shape=jax.ShapeDtypeStruct((m, n), jnp.bfloat16),
        compiler_params=_params(("parallel",)),
        name="norm_matmul",
    )(x, g[0], w[0])


def _memory_attention(q_mem, mk, mv):
    lane = lax.broadcasted_iota(jnp.int32, (1, MEM_WIDTH), 1)
    scale = 1.0 / (HEAD_DIM ** 0.5)
    out = jnp.zeros(q_mem.shape, jnp.float32)
    for h in range(MEM_HEADS):
        in_head = (lane >= h * HEAD_DIM) & (lane < (h + 1) * HEAD_DIM)
        qh = jnp.where(in_head, q_mem, 0.0).astype(jnp.bfloat16)
        s = lax.dot_general(qh, mk, _NT, preferred_element_type=jnp.float32) * scale
        p = jnp.exp(s - jnp.max(s, axis=-1, keepdims=True))
        l = jnp.sum(p, axis=-1, keepdims=True)
        vh = jnp.where(in_head, mv, jnp.zeros_like(mv))
        out = out + _dot(p.astype(jnp.bfloat16), vh) / l
    return out


def _out_projection(x, y_main, y_mem, wo_ref):
    main_width = y_main.shape[1]
    y = _dot(y_main.astype(jnp.bfloat16), wo_ref[:main_width, :])
    y = y + _dot(y_mem.astype(jnp.bfloat16), wo_ref[main_width:, :])
    return x + y


def _conv_mixer_kernel(x_ref, g_ref, win_ref, cw_ref, mk_ref, mv_ref, wo_ref, o_ref, tail_ref):
    main = cw_ref.shape[1]
    x = x_ref[...]
    h = _rmsnorm(x, g_ref[...]).astype(jnp.bfloat16)
    p = _dot(h, win_ref[...])
    b_gate = p[:, :main]
    cu = p[:, main:2 * main] * p[:, 2 * main:3 * main]
    q_mem = p[:, 3 * main:]

    @pl.when(pl.program_id(1) == 0)
    def _():
        tail_ref[...] = jnp.zeros_like(tail_ref)

    prev = tail_ref[...]
    rows = lax.broadcasted_iota(jnp.int32, prev.shape, 0)
    conv = cu * cw_ref[CONV_WIDTH - 1:CONV_WIDTH, :]
    for back in range(1, CONV_WIDTH):
        shifted = pltpu.roll(cu, back, 0)
        top = jnp.where(rows < back, pltpu.roll(prev, back, 0), shifted[:V7X_SUBLANES])
        shifted = jnp.concatenate([top, shifted[V7X_SUBLANES:]], axis=0)
        conv = conv + shifted * cw_ref[CONV_WIDTH - 1 - back:CONV_WIDTH - back, :]
    tail_ref[...] = cu[cu.shape[0] - V7X_SUBLANES:]

    y_main = b_gate * conv
    y_mem = _memory_attention(q_mem, mk_ref[...], mv_ref[...])
    o_ref[...] = _out_projection(x, y_main, y_mem, wo_ref)


def _conv_mixer(x, g, w_in, conv_w, mkv, layer, w_o, batch):
    m, d = x.shape
    main = conv_w[0].shape[2]
    seq = m // batch
    tm = ROW_TILE
    nt = seq // tm
    n_mem = mkv.shape[0] // batch
    return pl.pallas_call(
        _conv_mixer_kernel,
        grid=(batch, nt),
        in_specs=[pl.BlockSpec((tm, d), lambda b, t: (b * nt + t, 0)),
                  _layer_spec(g),
                  _layer_spec(w_in),
                  _layer_spec(conv_w),
                  pl.BlockSpec((n_mem, MEM_WIDTH), lambda b, t: (b, 2 * layer)),
                  pl.BlockSpec((n_mem, MEM_WIDTH), lambda b, t: (b, 2 * layer + 1)),
                  _layer_spec(w_o)],
        out_specs=pl.BlockSpec((tm, d), lambda b, t: (b * nt + t, 0)),
        out_shape=jax.ShapeDtypeStruct((m, d), jnp.float32),
        scratch_shapes=[pltpu.VMEM((V7X_SUBLANES, main), jnp.float32)],
        compiler_params=_params(("arbitrary", "arbitrary")),
        name="conv_mixer",
    )(x, g[0], w_in[0], conv_w[0], mkv, mkv, w_o[0])


LOG2E = 1.4426950408889634
MASKED_SCORE = -1e30


def _split_bf16(x):
    hi = x.astype(jnp.bfloat16)
    lo = (x - hi.astype(jnp.float32)).astype(jnp.bfloat16)
    return jnp.concatenate([hi, lo], axis=1)


SB_DONE_LOG2 = 160.0


def _sb_attention_kernel(q_ref, k_ref, v_ref, o_ref):
    tk = SB_TILE
    heads = range(V7X_LANES // HEAD_DIM)
    n_sub = q_ref.shape[0] // tk
    first_tile = pl.program_id(2) * n_sub
    lane = lax.broadcasted_iota(jnp.int32, (1, V7X_LANES), 1)
    in_head = [(lane >= h * HEAD_DIM) & (lane < (h + 1) * HEAD_DIM) for h in heads]
    r = lax.broadcasted_iota(jnp.int32, (tk, tk), 0)
    s = lax.broadcasted_iota(jnp.int32, (tk, tk), 1)
    suffix = jnp.where(r >= s, 1.0, 0.0).astype(jnp.bfloat16)
    suffix2 = jnp.concatenate([suffix, suffix], axis=0)
    strictly_before = r > s

    def queries(u):
        q = q_ref[u * tk:(u + 1) * tk, :]
        return [jnp.where(m, q, jnp.zeros_like(q)) * (HEAD_DIM ** -0.5) for m in in_head]

    def sweep(jobs):
        chains = [(u, n, h) for u, job in enumerate(jobs) for n in range(len(job[1])) for h in heads]
        kv = {}
        for u, job in enumerate(jobs):
            for n, (j, _) in enumerate(job[1]):
                rows = pl.ds(pl.multiple_of(j * tk, tk), tk)
                kv[u, n] = (k_ref[rows, :], v_ref[rows, :])
        y = {}
        for u, n, h in chains:
            yc = lax.dot_general(jobs[u][0][h], kv[u, n][0], _NT,
                                 preferred_element_type=jnp.float32) * LOG2E
            y[u, n, h] = jnp.where(strictly_before, yc, MASKED_SCORE) if jobs[u][1][n][1] else yc
        p = {c: jnp.maximum(y[c], 0.0) + jnp.log2(1.0 + jnp.exp2(-jnp.abs(y[c]))) for c in chains}
        total = {}
        c = [job[2] for job in jobs]
        for n in range(max(len(job[1]) for job in jobs)):
            for u, job in enumerate(jobs):
                if n >= len(job[1]):
                    continue
                for h in heads:
                    t = _dot(_split_bf16(p[u, n, h]), suffix2)
                    total[u, n, h] = t if c[u] is None else t + c[u][h]
                c[u] = [total[u, n, h][:, 0:1] for h in heads]
        acc = [list(job[3]) for job in jobs]
        for u, n, h in chains:
            a = jnp.exp2(y[u, n, h] - total[u, n, h])
            acc[u][h] = acc[u][h] + _dot(a.astype(jnp.bfloat16), kv[u, n][1])
        return [(jnp.minimum(jnp.min(c[u][0]), jnp.min(c[u][1])), c[u], acc[u])
                for u in range(len(jobs))]

    def store(u, acc):
        o_ref[u * tk:(u + 1) * tk, :] = jnp.where(in_head[0], acc[0], acc[1]).astype(o_ref.dtype)

    def finish(u, qh, next_block, state):
        def unfinished(state):
            return (state[0] >= 0) & (state[1] < SB_DONE_LOG2)

        def body(state):
            j, _, c, acc = state
            return (j - 1,) + sweep([(qh, [(j, False)], c, acc)])[0]

        store(u, lax.while_loop(unfinished, body, (next_block,) + state)[3])

    zeros = [jnp.zeros((tk, V7X_LANES), jnp.float32) for _ in heads]
    qs = [queries(u) for u in range(n_sub)]

    def first_blocks(tile, count):
        return [(tile, True)] + [(tile - i, False) for i in range(1, count)]

    @pl.when(first_tile == 0)
    def _():
        jobs = [(qs[u], first_blocks(u, min(u + 1, SB_FUSED_BLOCKS)), None, zeros)
                for u in range(n_sub)]
        for u, state in enumerate(sweep(jobs)):
            if u < SB_FUSED_BLOCKS:
                store(u, state[2])
            else:
                finish(u, qs[u], u - SB_FUSED_BLOCKS, state)

    @pl.when(first_tile > 0)
    def _():
        jobs = [(qs[u], first_blocks(first_tile + u, SB_FUSED_BLOCKS), None, zeros)
                for u in range(n_sub)]
        for u, state in enumerate(sweep(jobs)):
            finish(u, qs[u], first_tile + u - SB_FUSED_BLOCKS, state)


def _sb_attention(p, kv, batch, main):
    m = p.shape[0]
    seq = m // batch
    tq = SB_TILE * SB_Q_TILES
    nq = seq // tq
    n_pairs = main // V7X_LANES
    return pl.pallas_call(
        _sb_attention_kernel,
        grid=(batch, n_pairs, nq),
        in_specs=[pl.BlockSpec((tq, V7X_LANES), lambda b, hp, qi: (b * nq + qi, hp)),
                  pl.BlockSpec((seq, V7X_LANES), lambda b, hp, qi: (b, hp)),
                  pl.BlockSpec((seq, V7X_LANES), lambda b, hp, qi: (b, n_pairs + hp))],
        out_specs=pl.BlockSpec((tq, V7X_LANES), lambda b, hp, qi: (b * nq + qi, hp)),
        out_shape=jax.ShapeDtypeStruct((m, main), jnp.bfloat16),
        compiler_params=_params(("parallel", "parallel", "parallel")),
        name="sb_attention",
    )(p, kv, kv)


def _attn_out_kernel(x_ref, ymain_ref, qmem_ref, mk_ref, mv_ref, wo_ref, o_ref):
    y_mem = _memory_attention(qmem_ref[...].astype(jnp.float32), mk_ref[...], mv_ref[...])
    o_ref[...] = _out_projection(x_ref[...], ymain_ref[...], y_mem, wo_ref)


def _attn_out(x, y_main, p, mkv, layer, w_o, batch):
    m, d = x.shape
    main = y_main.shape[1]
    tm = ROW_TILE
    nt = m // batch // tm
    n_mem = mkv.shape[0] // batch
    return pl.pallas_call(
        _attn_out_kernel,
        grid=(batch, nt),
        in_specs=[pl.BlockSpec((tm, d), lambda b, t: (b * nt + t, 0)),
                  pl.BlockSpec((tm, main), lambda b, t: (b * nt + t, 0)),
                  pl.BlockSpec((tm, MEM_WIDTH), lambda b, t: (b * nt + t, main // MEM_WIDTH)),
                  pl.BlockSpec((n_mem, MEM_WIDTH), lambda b, t: (b, 2 * layer)),
                  pl.BlockSpec((n_mem, MEM_WIDTH), lambda b, t: (b, 2 * layer + 1)),
                  _layer_spec(w_o)],
        out_specs=pl.BlockSpec((tm, d), lambda b, t: (b * nt + t, 0)),
        out_shape=jax.ShapeDtypeStruct((m, d), jnp.float32),
        compiler_params=_params(("parallel", "parallel")),
        name="attn_out",
    )(x, y_main, p, mkv, mkv, w_o[0])


def _ffn_kernel(x_ref, g_ref, wg_ref, wu_ref, wd_ref, fg_ref, o_ref, *, final_norm):
    x = x_ref[...]
    h = _rmsnorm(x, g_ref[...]).astype(jnp.bfloat16)
    d_ff = wg_ref.shape[1]
    chunk = pl.cdiv(d_ff // V7X_MXU_WIDTH, FFN_CHUNKS) * V7X_MXU_WIDTH
    y = x
    for start in range(0, d_ff, chunk):
        cols = slice(start, min(start + chunk, d_ff))
        gate = _dot(h, wg_ref[:, cols])
        up = _dot(h, wu_ref[:, cols])
        act = (gate / (1.0 + jnp.exp(-gate))) * up
        y = y + _dot(act.astype(jnp.bfloat16), wd_ref[cols, :])
    if final_norm:
        y = _rmsnorm(y, fg_ref[...])
    o_ref[...] = y


def _ffn(x, g, w_gate, w_up, w_down, final_g, final_norm):
    m, d = x.shape
    d_ff = w_gate[0].shape[2]
    assert d_ff % V7X_MXU_WIDTH == 0
    tm = ROW_TILE
    return pl.pallas_call(
        partial(_ffn_kernel, final_norm=final_norm),
        grid=(m // tm,),
        in_specs=[pl.BlockSpec((tm, d), lambda i: (i, 0)),
                  _layer_spec(g),
                  _layer_spec(w_gate),
                  _layer_spec(w_up),
                  _layer_spec(w_down),
                  _layer_spec(final_g)],
        out_specs=pl.BlockSpec((tm, d), lambda i: (i, 0)),
        out_shape=jax.ShapeDtypeStruct((m, d), jnp.float32),
        compiler_params=_params(("parallel",)),
        name="ffn",
    )(x, g[0], w_gate[0], w_up[0], w_down[0], final_g[0])


def kernel(x, mem, mix_norm, a_in, conv_w, b_in, kv_norm, w_kv_shared, w_mem_kv, w_o, ffn_norm,
           w_gate, w_up, w_down, mem_norm, final_norm):
    batch, seq, d = x.shape
    depth = mix_norm.shape[0]
    n_a = a_in.shape[0]
    main = conv_w.shape[2]
    bf16 = jnp.bfloat16
    assert seq % ROW_TILE == 0 and seq % (SB_TILE * SB_Q_TILES) == 0 and main % V7X_LANES == 0
    assert SB_Q_TILES >= SB_FUSED_BLOCKS - 1

    xs = x.reshape(batch * seq, d)

    def gains(g):
        return g.reshape(-1, 1, d)

    mix_g, ffn_g = gains(mix_norm), gains(ffn_norm)
    w_in_a, w_in_b, wo = a_in.astype(bf16), b_in.astype(bf16), w_o.astype(bf16)
    wg, wu, wd = w_gate.astype(bf16), w_up.astype(bf16), w_down.astype(bf16)
    w_kv = w_kv_shared.astype(bf16)[None]
    w_mkv = jnp.transpose(w_mem_kv, (1, 0, 2)).reshape(1, d, depth * 2 * MEM_WIDTH).astype(bf16)
    mkv = _norm_matmul(mem.reshape(-1, d), (gains(mem_norm), 0), (w_mkv, 0))

    kv = None
    for i in range(depth):
        if i < n_a:
            xs = _conv_mixer(xs, (mix_g, i), (w_in_a, i), (conv_w, i), mkv, i, (wo, i), batch)
        else:
            p = _norm_matmul(xs, (mix_g, i), (w_in_b, i - n_a))
            y_main = _sb_attention(p, kv, batch, main)
            xs = _attn_out(xs, y_main, p, mkv, i, (wo, i), batch)
        last = i == depth - 1
        xs = _ffn(xs, (ffn_g, i), (wg, i), (wu, i), (wd, i), (gains(final_norm), 0), last)
        if i == n_a - 1:
            kv = _norm_matmul(xs, (gains(kv_norm), 0), (w_kv, 0))
    return xs.reshape(batch, seq, d)
```

```python
from functools import partial

import jax
import jax.numpy as jnp
from jax import lax
from jax.experimental import pallas as pl
from jax.experimental.pallas import tpu as pltpu

HEAD_DIM = 64
MEM_HEADS = 4
MEM_WIDTH = MEM_HEADS * HEAD_DIM
CONV_WIDTH = 3
EPS = 1e-6

V7X_LANES = 128
V7X_SUBLANES = 8
V7X_MXU_WIDTH = 256
V7X_VMEM_LIMIT_BYTES = 56 * 1024 * 1024

ROW_TILE = 1024
SB_TILE = 128
SB_Q_TILES = 4
SB_FUSED_BLOCKS = 3
FFN_CHUNKS = 2
FFN_ROW_CHUNKS = 2

_NT = (((1,), (1,)), ((), ()))


def _rmsnorm(x, g):
    y = x * lax.rsqrt(jnp.mean(x * x, axis=-1, keepdims=True) + EPS)
    return y * g


def _dot(a, b):
    return jnp.dot(a, b, preferred_element_type=jnp.float32)


def _layer_spec(param):
    stacked, layer = param
    shape = stacked.shape[1:]
    return pl.BlockSpec((None,) + shape, lambda *_: (layer,) + (0,) * len(shape),
                        pipeline_mode=pl.Buffered(1))


def _params(semantics):
    return pltpu.CompilerParams(dimension_semantics=semantics,
                                vmem_limit_bytes=V7X_VMEM_LIMIT_BYTES)


def _norm_matmul_kernel(x_ref, g_ref, w_ref, o_ref):
    h = _rmsnorm(x_ref[...], g_ref[...]).astype(jnp.bfloat16)
    o_ref[...] = _dot(h, w_ref[...]).astype(o_ref.dtype)


def _norm_matmul(x, g, w):
    m, d = x.shape
    n = w[0].shape[2]
    tm = min(ROW_TILE, m)
    return pl.pallas_call(
        _norm_matmul_kernel,
        grid=(m // tm,),
        in_specs=[pl.BlockSpec((tm, d), lambda i: (i, 0)),
                  _layer_spec(g),
                  _layer_spec(w)],
        out_specs=pl.BlockSpec((tm, n), lambda i: (i, 0)),
        out_shape=jax.ShapeDtypeStruct((m, n), jnp.bfloat16),
        compiler_params=_params(("parallel",)),
        name="norm_matmul",
    )(x, g[0], w[0])


def _memory_attention(q_mem, mk, mv):
    lane = lax.broadcasted_iota(jnp.int32, (1, MEM_WIDTH), 1)
    scale = 1.0 / (HEAD_DIM ** 0.5)
    out = jnp.zeros(q_mem.shape, jnp.float32)
    for h in range(MEM_HEADS):
        in_head = (lane >= h * HEAD_DIM) & (lane < (h + 1) * HEAD_DIM)
        qh = jnp.where(in_head, q_mem, 0.0).astype(jnp.bfloat16)
        s = lax.dot_general(qh, mk, _NT, preferred_element_type=jnp.float32) * scale
        p = jnp.exp(s - jnp.max(s, axis=-1, keepdims=True))
        l = jnp.sum(p, axis=-1, keepdims=True)
        vh = jnp.where(in_head, mv, jnp.zeros_like(mv))
        out = out + _dot(p.astype(jnp.bfloat16), vh) / l
    return out


def _out_projection(x, y_main, y_mem, wo_ref):
    main_width = y_main.shape[1]
    y = _dot(y_main.astype(jnp.bfloat16), wo_ref[:main_width, :])
    y = y + _dot(y_mem.astype(jnp.bfloat16), wo_ref[main_width:, :])
    return x + y


def _conv_mixer_kernel(x_ref, g_ref, win_ref, cw_ref, mk_ref, mv_ref, wo_ref, o_ref, tail_ref):
    main = cw_ref.shape[1]
    x = x_ref[...]
    h = _rmsnorm(x, g_ref[...]).astype(jnp.bfloat16)
    p = _dot(h, win_ref[...])
    b_gate = p[:, :main]
    cu = p[:, main:2 * main] * p[:, 2 * main:3 * main]
    q_mem = p[:, 3 * main:]

    @pl.when(pl.program_id(1) == 0)
    def _():
        tail_ref[...] = jnp.zeros_like(tail_ref)

    prev = tail_ref[...]
    rows = lax.broadcasted_iota(jnp.int32, prev.shape, 0)
    conv = cu * cw_ref[CONV_WIDTH - 1:CONV_WIDTH, :]
    for back in range(1, CONV_WIDTH):
        shifted = pltpu.roll(cu, back, 0)
        top = jnp.where(rows < back, pltpu.roll(prev, back, 0), shifted[:V7X_SUBLANES])
        shifted = jnp.concatenate([top, shifted[V7X_SUBLANES:]], axis=0)
        conv = conv + shifted * cw_ref[CONV_WIDTH - 1 - back:CONV_WIDTH - back, :]
    tail_ref[...] = cu[cu.shape[0] - V7X_SUBLANES:]

    y_main = b_gate * conv
    y_mem = _memory_attention(q_mem, mk_ref[...], mv_ref[...])
    o_ref[...] = _out_projection(x, y_main, y_mem, wo_ref)


def _conv_mixer(x, g, w_in, conv_w, mkv, layer, w_o, batch):
    m, d = x.shape
    main = conv_w[0].shape[2]
    seq = m // batch
    tm = ROW_TILE
    nt = seq // tm
    n_mem = mkv.shape[0] // batch
    return pl.pallas_call(
        _conv_mixer_kernel,
        grid=(batch, nt),
        in_specs=[pl.BlockSpec((tm, d), lambda b, t: (b * nt + t, 0)),
                  _layer_spec(g),
                  _layer_spec(w_in),
                  _layer_spec(conv_w),
                  pl.BlockSpec((n_mem, MEM_WIDTH), lambda b, t: (b, 2 * layer)),
                  pl.BlockSpec((n_mem, MEM_WIDTH), lambda b, t: (b, 2 * layer + 1)),
                  _layer_spec(w_o)],
        out_specs=pl.BlockSpec((tm, d), lambda b, t: (b * nt + t, 0)),
        out_shape=jax.ShapeDtypeStruct((m, d), jnp.float32),
        scratch_shapes=[pltpu.VMEM((V7X_SUBLANES, main), jnp.float32)],
        compiler_params=_params(("arbitrary", "arbitrary")),
        name="conv_mixer",
    )(x, g[0], w_in[0], conv_w[0], mkv, mkv, w_o[0])


LOG2E = 1.4426950408889634
MASKED_SCORE = -1e30


def _split_bf16(x):
    hi = x.astype(jnp.bfloat16)
    lo = (x - hi.astype(jnp.float32)).astype(jnp.bfloat16)
    return jnp.concatenate([hi, lo], axis=1)


SB_DONE_LOG2 = 160.0


def _sb_attention_kernel(q_ref, k_ref, v_ref, o_ref):
    tk = SB_TILE
    heads = range(V7X_LANES // HEAD_DIM)
    n_sub = q_ref.shape[0] // tk
    first_tile = pl.program_id(2) * n_sub
    lane = lax.broadcasted_iota(jnp.int32, (1, V7X_LANES), 1)
    in_head = [(lane >= h * HEAD_DIM) & (lane < (h + 1) * HEAD_DIM) for h in heads]
    r = lax.broadcasted_iota(jnp.int32, (tk, tk), 0)
    s = lax.broadcasted_iota(jnp.int32, (tk, tk), 1)
    suffix = jnp.where(r >= s, 1.0, 0.0).astype(jnp.bfloat16)
    suffix2 = jnp.concatenate([suffix, suffix], axis=0)
    strictly_before = r > s

    def queries(u):
        q = q_ref[u * tk:(u + 1) * tk, :]
        return [jnp.where(m, q, jnp.zeros_like(q)) * (HEAD_DIM ** -0.5) for m in in_head]

    def sweep(jobs):
        chains = [(u, n, h) for u, job in enumerate(jobs) for n in range(len(job[1])) for h in heads]
        kv = {}
        for u, job in enumerate(jobs):
            for n, (j, _) in enumerate(job[1]):
                rows = pl.ds(pl.multiple_of(j * tk, tk), tk)
                kv[u, n] = (k_ref[rows, :], v_ref[rows, :])
        y = {}
        for u, n, h in chains:
            yc = lax.dot_general(jobs[u][0][h], kv[u, n][0], _NT,
                                 preferred_element_type=jnp.float32) * LOG2E
            y[u, n, h] = jnp.where(strictly_before, yc, MASKED_SCORE) if jobs[u][1][n][1] else yc
        p = {c: jnp.maximum(y[c], 0.0) + jnp.log2(1.0 + jnp.exp2(-jnp.abs(y[c]))) for c in chains}
        total = {}
        c = [job[2] for job in jobs]
        for n in range(max(len(job[1]) for job in jobs)):
            for u, job in enumerate(jobs):
                if n >= len(job[1]):
                    continue
                for h in heads:
                    t = _dot(_split_bf16(p[u, n, h]), suffix2)
                    total[u, n, h] = t if c[u] is None else t + c[u][h]
                c[u] = [total[u, n, h][:, 0:1] for h in heads]
        acc = [list(job[3]) for job in jobs]
        for u, n, h in chains:
            a = jnp.exp2(y[u, n, h] - total[u, n, h])
            acc[u][h] = acc[u][h] + _dot(a.astype(jnp.bfloat16), kv[u, n][1])
        return [(jnp.minimum(jnp.min(c[u][0]), jnp.min(c[u][1])), c[u], acc[u])
                for u in range(len(jobs))]

    def store(u, acc):
        o_ref[u * tk:(u + 1) * tk, :] = jnp.where(in_head[0], acc[0], acc[1]).astype(o_ref.dtype)

    def finish(u, qh, next_block, state):
        def unfinished(state):
            return (state[0] >= 0) & (state[1] < SB_DONE_LOG2)

        def body(state):
            j, _, c, acc = state
            return (j - 1,) + sweep([(qh, [(j, False)], c, acc)])[0]

        store(u, lax.while_loop(unfinished, body, (next_block,) + state)[3])

    zeros = [jnp.zeros((tk, V7X_LANES), jnp.float32) for _ in heads]
    qs = [queries(u) for u in range(n_sub)]

    def first_blocks(tile, count):
        return [(tile, True)] + [(tile - i, False) for i in range(1, count)]

    @pl.when(first_tile == 0)
    def _():
        jobs = [(qs[u], first_blocks(u, min(u + 1, SB_FUSED_BLOCKS)), None, zeros)
                for u in range(n_sub)]
        for u, state in enumerate(sweep(jobs)):
            if u < SB_FUSED_BLOCKS:
                store(u, state[2])
            else:
                finish(u, qs[u], u - SB_FUSED_BLOCKS, state)

    @pl.when(first_tile > 0)
    def _():
        jobs = [(qs[u], first_blocks(first_tile + u, SB_FUSED_BLOCKS), None, zeros)
                for u in range(n_sub)]
        for u, state in enumerate(sweep(jobs)):
            finish(u, qs[u], first_tile + u - SB_FUSED_BLOCKS, state)


def _sb_attention(p, kv, batch, main):
    m = p.shape[0]
    seq = m // batch
    tq = SB_TILE * SB_Q_TILES
    nq = seq // tq
    n_pairs = main // V7X_LANES
    return pl.pallas_call(
        _sb_attention_kernel,
        grid=(batch, n_pairs, nq),
        in_specs=[pl.BlockSpec((tq, V7X_LANES), lambda b, hp, qi: (b * nq + qi, hp)),
                  pl.BlockSpec((seq, V7X_LANES), lambda b, hp, qi: (b, hp)),
                  pl.BlockSpec((seq, V7X_LANES), lambda b, hp, qi: (b, n_pairs + hp))],
        out_specs=pl.BlockSpec((tq, V7X_LANES), lambda b, hp, qi: (b * nq + qi, hp)),
        out_shape=jax.ShapeDtypeStruct((m, main), jnp.bfloat16),
        compiler_params=_params(("parallel", "parallel", "parallel")),
        name="sb_attention",
    )(p, kv, kv)


def _attn_out_kernel(x_ref, ymain_ref, qmem_ref, mk_ref, mv_ref, wo_ref, o_ref):
    y_mem = _memory_attention(qmem_ref[...].astype(jnp.float32), mk_ref[...], mv_ref[...])
    o_ref[...] = _out_projection(x_ref[...], ymain_ref[...], y_mem, wo_ref)


def _attn_out(x, y_main, p, mkv, layer, w_o, batch):
    m, d = x.shape
    main = y_main.shape[1]
    tm = ROW_TILE
    nt = m // batch // tm
    n_mem = mkv.shape[0] // batch
    return pl.pallas_call(
        _attn_out_kernel,
        grid=(batch, nt),
        in_specs=[pl.BlockSpec((tm, d), lambda b, t: (b * nt + t, 0)),
                  pl.BlockSpec((tm, main), lambda b, t: (b * nt + t, 0)),
                  pl.BlockSpec((tm, MEM_WIDTH), lambda b, t: (b * nt + t, main // MEM_WIDTH)),
                  pl.BlockSpec((n_mem, MEM_WIDTH), lambda b, t: (b, 2 * layer)),
                  pl.BlockSpec((n_mem, MEM_WIDTH), lambda b, t: (b, 2 * layer + 1)),
                  _layer_spec(w_o)],
        out_specs=pl.BlockSpec((tm, d), lambda b, t: (b * nt + t, 0)),
        out_shape=jax.ShapeDtypeStruct((m, d), jnp.float32),
        compiler_params=_params(("parallel", "parallel")),
        name="attn_out",
    )(x, y_main, p, mkv, mkv, w_o[0])


def _ffn_kernel(*refs, final_norm, n_proj):
    x_ref, g_ref, wg_ref, wu_ref, wd_ref, fg_ref = refs[:6]
    proj_refs = refs[6:6 + 2 * n_proj]
    o_ref = refs[6 + 2 * n_proj]
    proj_out_refs = refs[7 + 2 * n_proj:]
    d_ff = wg_ref.shape[1]
    chunk = pl.cdiv(d_ff // V7X_MXU_WIDTH, FFN_CHUNKS) * V7X_MXU_WIDTH
    rows = x_ref.shape[0] // FFN_ROW_CHUNKS
    for c in range(FFN_ROW_CHUNKS):
        part = slice(c * rows, (c + 1) * rows)
        x = x_ref[part, :]
        h = _rmsnorm(x, g_ref[...]).astype(jnp.bfloat16)
        y = x
        for start in range(0, d_ff, chunk):
            cols = slice(start, min(start + chunk, d_ff))
            gate = _dot(h, wg_ref[:, cols])
            up = _dot(h, wu_ref[:, cols])
            act = (gate / (1.0 + jnp.exp(-gate))) * up
            y = y + _dot(act.astype(jnp.bfloat16), wd_ref[cols, :])
        for k in range(n_proj):
            hk = _rmsnorm(y, proj_refs[2 * k][...]).astype(jnp.bfloat16)
            proj_out_refs[k][part, :] = _dot(hk, proj_refs[2 * k + 1][...]).astype(jnp.bfloat16)
        if final_norm:
            y = _rmsnorm(y, fg_ref[...])
        o_ref[part, :] = y


def _ffn(x, g, w_gate, w_up, w_down, final_g, final_norm, projections=()):
    m, d = x.shape
    d_ff = w_gate[0].shape[2]
    assert d_ff % V7X_MXU_WIDTH == 0
    tm = ROW_TILE

    def row_block(n):
        return pl.BlockSpec((tm, n), lambda i: (i, 0))

    proj_widths = [w[0].shape[2] for _, w in projections]
    outs = pl.pallas_call(
        partial(_ffn_kernel, final_norm=final_norm, n_proj=len(projections)),
        grid=(m // tm,),
        in_specs=[row_block(d),
                  _layer_spec(g),
                  _layer_spec(w_gate),
                  _layer_spec(w_up),
                  _layer_spec(w_down),
                  _layer_spec(final_g)] + [_layer_spec(p) for pair in projections for p in pair],
        out_specs=[row_block(d)] + [row_block(n) for n in proj_widths],
        out_shape=[jax.ShapeDtypeStruct((m, d), jnp.float32)]
        + [jax.ShapeDtypeStruct((m, n), jnp.bfloat16) for n in proj_widths],
        compiler_params=_params(("parallel",)),
        name="ffn",
    )(x, g[0], w_gate[0], w_up[0], w_down[0], final_g[0],
      *[p[0] for pair in projections for p in pair])
    return outs[0], outs[1:]


def kernel(x, mem, mix_norm, a_in, conv_w, b_in, kv_norm, w_kv_shared, w_mem_kv, w_o, ffn_norm,
           w_gate, w_up, w_down, mem_norm, final_norm):
    batch, seq, d = x.shape
    depth = mix_norm.shape[0]
    n_a = a_in.shape[0]
    main = conv_w.shape[2]
    bf16 = jnp.bfloat16
    assert seq % ROW_TILE == 0 and seq % (SB_TILE * SB_Q_TILES) == 0 and main % V7X_LANES == 0
    assert SB_Q_TILES >= SB_FUSED_BLOCKS - 1
    assert 1 <= n_a < depth

    xs = x.reshape(batch * seq, d)

    def gains(g):
        return g.reshape(-1, 1, d)

    mix_g, ffn_g = gains(mix_norm), gains(ffn_norm)
    w_in_a, w_in_b, wo = a_in.astype(bf16), b_in.astype(bf16), w_o.astype(bf16)
    wg, wu, wd = w_gate.astype(bf16), w_up.astype(bf16), w_down.astype(bf16)
    w_kv = w_kv_shared.astype(bf16)[None]
    w_mkv = jnp.transpose(w_mem_kv, (1, 0, 2)).reshape(1, d, depth * 2 * MEM_WIDTH).astype(bf16)
    mkv = _norm_matmul(mem.reshape(-1, d), (gains(mem_norm), 0), (w_mkv, 0))

    kv = p = None
    for i in range(depth):
        if i < n_a:
            xs = _conv_mixer(xs, (mix_g, i), (w_in_a, i), (conv_w, i), mkv, i, (wo, i), batch)
        else:
            y_main = _sb_attention(p, kv, batch, main)
            xs = _attn_out(xs, y_main, p, mkv, i, (wo, i), batch)
        projections = []
        if i == n_a - 1:
            projections.append(((gains(kv_norm), 0), (w_kv, 0)))
        if n_a - 1 <= i < depth - 1:
            projections.append(((mix_g, i + 1), (w_in_b, i + 1 - n_a)))
        xs, extra = _ffn(xs, (ffn_g, i), (wg, i), (wu, i), (wd, i), (gains(final_norm), 0),
                         i == depth - 1, projections)
        if i == n_a - 1:
            kv = extra[0]
        if projections:
            p = extra[-1]
    return xs.reshape(batch, seq, d)
```

```python
from functools import partial

import jax
import jax.numpy as jnp
from jax import lax
from jax.experimental import pallas as pl
from jax.experimental.pallas import tpu as pltpu

HEAD_DIM = 64
MEM_HEADS = 4
MEM_WIDTH = MEM_HEADS * HEAD_DIM
CONV_WIDTH = 3
EPS = 1e-6

V7X_LANES = 128
V7X_SUBLANES = 8
V7X_MXU_WIDTH = 256
V7X_VMEM_LIMIT_BYTES = 56 * 1024 * 1024

ROW_TILE = 1024
SB_TILE = 128
SB_Q_TILES = 4
SB_FUSED_BLOCKS = 3
FFN_CHUNKS = 2
FFN_ROW_CHUNKS = 2

_NT = (((1,), (1,)), ((), ()))


def _rmsnorm(x, g):
    y = x * lax.rsqrt(jnp.mean(x * x, axis=-1, keepdims=True) + EPS)
    return y * g


def _dot(a, b):
    return jnp.dot(a, b, preferred_element_type=jnp.float32)


def _layer_spec(param):
    stacked, layer = param
    shape = stacked.shape[1:]
    return pl.BlockSpec((None,) + shape, lambda *_: (layer,) + (0,) * len(shape),
                        pipeline_mode=pl.Buffered(1))


def _params(semantics):
    return pltpu.CompilerParams(dimension_semantics=semantics,
                                vmem_limit_bytes=V7X_VMEM_LIMIT_BYTES)


def _norm_matmul_kernel(x_ref, g_ref, w_ref, o_ref):
    h = _rmsnorm(x_ref[...], g_ref[...]).astype(jnp.bfloat16)
    o_ref[...] = _dot(h, w_ref[...]).astype(o_ref.dtype)


def _memory_kv(mem, g, w):
    m, d = mem.shape
    layers, _, n = w.shape
    return pl.pallas_call(
        _norm_matmul_kernel,
        grid=(layers,),
        in_specs=[pl.BlockSpec((m, d), lambda l: (0, 0)),
                  _layer_spec(g),
                  pl.BlockSpec((None, d, n), lambda l: (l, 0, 0))],
        out_specs=pl.BlockSpec((None, m, n), lambda l: (l, 0, 0)),
        out_shape=jax.ShapeDtypeStruct((layers, m, n), jnp.bfloat16),
        compiler_params=_params(("parallel",)),
        name="memory_kv",
    )(mem, g[0], w)


def _memory_attention(q_mem, mk, mv):
    lane = lax.broadcasted_iota(jnp.int32, (1, MEM_WIDTH), 1)
    scale = 1.0 / (HEAD_DIM ** 0.5)
    out = jnp.zeros(q_mem.shape, jnp.float32)
    for h in range(MEM_HEADS):
        in_head = (lane >= h * HEAD_DIM) & (lane < (h + 1) * HEAD_DIM)
        qh = jnp.where(in_head, q_mem, 0.0).astype(jnp.bfloat16)
        s = lax.dot_general(qh, mk, _NT, preferred_element_type=jnp.float32) * scale
        p = jnp.exp(s - jnp.max(s, axis=-1, keepdims=True))
        l = jnp.sum(p, axis=-1, keepdims=True)
        vh = jnp.where(in_head, mv, jnp.zeros_like(mv))
        out = out + _dot(p.astype(jnp.bfloat16), vh) / l
    return out


def _out_projection(x, y_main, y_mem, wo_ref):
    main_width = y_main.shape[1]
    y = _dot(y_main.astype(jnp.bfloat16), wo_ref[:main_width, :])
    y = y + _dot(y_mem.astype(jnp.bfloat16), wo_ref[main_width:, :])
    return x + y


def _conv_mixer_kernel(x_ref, g_ref, win_ref, cw_ref, mk_ref, mv_ref, wo_ref, o_ref, tail_ref):
    main = cw_ref.shape[1]
    x = x_ref[...]
    h = _rmsnorm(x, g_ref[...]).astype(jnp.bfloat16)
    p = _dot(h, win_ref[...])
    b_gate = p[:, :main]
    cu = p[:, main:2 * main] * p[:, 2 * main:3 * main]
    q_mem = p[:, 3 * main:]

    @pl.when(pl.program_id(1) == 0)
    def _():
        tail_ref[...] = jnp.zeros_like(tail_ref)

    prev = tail_ref[...]
    rows = lax.broadcasted_iota(jnp.int32, prev.shape, 0)
    conv = cu * cw_ref[CONV_WIDTH - 1:CONV_WIDTH, :]
    for back in range(1, CONV_WIDTH):
        shifted = pltpu.roll(cu, back, 0)
        top = jnp.where(rows < back, pltpu.roll(prev, back, 0), shifted[:V7X_SUBLANES])
        shifted = jnp.concatenate([top, shifted[V7X_SUBLANES:]], axis=0)
        conv = conv + shifted * cw_ref[CONV_WIDTH - 1 - back:CONV_WIDTH - back, :]
    tail_ref[...] = cu[cu.shape[0] - V7X_SUBLANES:]

    y_main = b_gate * conv
    y_mem = _memory_attention(q_mem, mk_ref[...], mv_ref[...])
    o_ref[...] = _out_projection(x, y_main, y_mem, wo_ref)


def _conv_mixer(x, g, w_in, conv_w, mkv, layer, w_o, batch):
    m, d = x.shape
    main = conv_w[0].shape[2]
    seq = m // batch
    tm = ROW_TILE
    nt = seq // tm
    n_mem = mkv.shape[1] // batch
    return pl.pallas_call(
        _conv_mixer_kernel,
        grid=(batch, nt),
        in_specs=[pl.BlockSpec((tm, d), lambda b, t: (b * nt + t, 0)),
                  _layer_spec(g),
                  _layer_spec(w_in),
                  _layer_spec(conv_w),
                  pl.BlockSpec((None, n_mem, MEM_WIDTH), lambda b, t: (layer, b, 0)),
                  pl.BlockSpec((None, n_mem, MEM_WIDTH), lambda b, t: (layer, b, 1)),
                  _layer_spec(w_o)],
        out_specs=pl.BlockSpec((tm, d), lambda b, t: (b * nt + t, 0)),
        out_shape=jax.ShapeDtypeStruct((m, d), jnp.float32),
        scratch_shapes=[pltpu.VMEM((V7X_SUBLANES, main), jnp.float32)],
        compiler_params=_params(("arbitrary", "arbitrary")),
        name="conv_mixer",
    )(x, g[0], w_in[0], conv_w[0], mkv, mkv, w_o[0])


LOG2E = 1.4426950408889634
MASKED_SCORE = -1e30


def _split_bf16(x):
    hi = x.astype(jnp.bfloat16)
    lo = (x - hi.astype(jnp.float32)).astype(jnp.bfloat16)
    return jnp.concatenate([hi, lo], axis=1)


SB_DONE_LOG2 = 160.0


def _sb_attention_kernel(q_ref, k_ref, v_ref, o_ref):
    tk = SB_TILE
    heads = range(V7X_LANES // HEAD_DIM)
    n_sub = q_ref.shape[0] // tk
    first_tile = pl.program_id(2) * n_sub
    lane = lax.broadcasted_iota(jnp.int32, (1, V7X_LANES), 1)
    in_head = [(lane >= h * HEAD_DIM) & (lane < (h + 1) * HEAD_DIM) for h in heads]
    r = lax.broadcasted_iota(jnp.int32, (tk, tk), 0)
    s = lax.broadcasted_iota(jnp.int32, (tk, tk), 1)
    suffix = jnp.where(r >= s, 1.0, 0.0).astype(jnp.bfloat16)
    suffix2 = jnp.concatenate([suffix, suffix], axis=0)
    strictly_before = r > s

    def queries(u):
        q = q_ref[u * tk:(u + 1) * tk, :]
        return [jnp.where(m, q, jnp.zeros_like(q)) * (HEAD_DIM ** -0.5) for m in in_head]

    def sweep(jobs):
        chains = [(u, n, h) for u, job in enumerate(jobs) for n in range(len(job[1])) for h in heads]
        kv = {}
        for u, job in enumerate(jobs):
            for n, (j, _) in enumerate(job[1]):
                rows = pl.ds(pl.multiple_of(j * tk, tk), tk)
                kv[u, n] = (k_ref[rows, :], v_ref[rows, :])
        y = {}
        for u, n, h in chains:
            yc = lax.dot_general(jobs[u][0][h], kv[u, n][0], _NT,
                                 preferred_element_type=jnp.float32) * LOG2E
            y[u, n, h] = jnp.where(strictly_before, yc, MASKED_SCORE) if jobs[u][1][n][1] else yc
        p = {c: jnp.maximum(y[c], 0.0) + jnp.log2(1.0 + jnp.exp2(-jnp.abs(y[c]))) for c in chains}
        total = {}
        c = [job[2] for job in jobs]
        for n in range(max(len(job[1]) for job in jobs)):
            for u, job in enumerate(jobs):
                if n >= len(job[1]):
                    continue
                for h in heads:
                    t = _dot(_split_bf16(p[u, n, h]), suffix2)
                    total[u, n, h] = t if c[u] is None else t + c[u][h]
                c[u] = [total[u, n, h][:, 0:1] for h in heads]
        acc = [list(job[3]) for job in jobs]
        for u, n, h in chains:
            a = jnp.exp2(y[u, n, h] - total[u, n, h])
            acc[u][h] = acc[u][h] + _dot(a.astype(jnp.bfloat16), kv[u, n][1])
        return [(jnp.minimum(jnp.min(c[u][0]), jnp.min(c[u][1])), c[u], acc[u])
                for u in range(len(jobs))]

    def store(u, acc):
        o_ref[u * tk:(u + 1) * tk, :] = jnp.where(in_head[0], acc[0], acc[1]).astype(o_ref.dtype)

    def finish(u, qh, next_block, state):
        def unfinished(state):
            return (state[0] >= 0) & (state[1] < SB_DONE_LOG2)

        def body(state):
            j, _, c, acc = state
            return (j - 1,) + sweep([(qh, [(j, False)], c, acc)])[0]

        store(u, lax.while_loop(unfinished, body, (next_block,) + state)[3])

    zeros = [jnp.zeros((tk, V7X_LANES), jnp.float32) for _ in heads]
    qs = [queries(u) for u in range(n_sub)]

    def first_blocks(tile, count):
        return [(tile, True)] + [(tile - i, False) for i in range(1, count)]

    @pl.when(first_tile == 0)
    def _():
        jobs = [(qs[u], first_blocks(u, min(u + 1, SB_FUSED_BLOCKS)), None, zeros)
                for u in range(n_sub)]
        for u, state in enumerate(sweep(jobs)):
            if u < SB_FUSED_BLOCKS:
                store(u, state[2])
            else:
                finish(u, qs[u], u - SB_FUSED_BLOCKS, state)

    @pl.when(first_tile > 0)
    def _():
        jobs = [(qs[u], first_blocks(first_tile + u, SB_FUSED_BLOCKS), None, zeros)
                for u in range(n_sub)]
        for u, state in enumerate(sweep(jobs)):
            finish(u, qs[u], first_tile + u - SB_FUSED_BLOCKS, state)


def _sb_attention(p, kv, batch, main):
    m = p.shape[0]
    seq = m // batch
    tq = SB_TILE * SB_Q_TILES
    nq = seq // tq
    n_pairs = main // V7X_LANES
    return pl.pallas_call(
        _sb_attention_kernel,
        grid=(batch, n_pairs, nq),
        in_specs=[pl.BlockSpec((tq, V7X_LANES), lambda b, hp, qi: (b * nq + qi, hp)),
                  pl.BlockSpec((seq, V7X_LANES), lambda b, hp, qi: (b, hp)),
                  pl.BlockSpec((seq, V7X_LANES), lambda b, hp, qi: (b, n_pairs + hp))],
        out_specs=pl.BlockSpec((tq, V7X_LANES), lambda b, hp, qi: (b * nq + qi, hp)),
        out_shape=jax.ShapeDtypeStruct((m, main), jnp.bfloat16),
        compiler_params=_params(("parallel", "parallel", "parallel")),
        name="sb_attention",
    )(p, kv, kv)


def _ffn_kernel(*refs, attention_out, final_norm, n_proj):
    x_ref, g_ref, wg_ref, wu_ref, wd_ref, fg_ref = refs[:6]
    refs = refs[6:]
    if attention_out:
        ymain_ref, qmem_ref, mk_ref, mv_ref, wo_ref = refs[:5]
        refs = refs[5:]
    proj_refs = refs[:2 * n_proj]
    o_ref = refs[2 * n_proj]
    proj_out_refs = refs[2 * n_proj + 1:]
    d_ff = wg_ref.shape[1]
    chunk = pl.cdiv(d_ff // V7X_MXU_WIDTH, FFN_CHUNKS) * V7X_MXU_WIDTH
    rows = x_ref.shape[0] // FFN_ROW_CHUNKS
    if attention_out:
        y_mem = _memory_attention(qmem_ref[...].astype(jnp.float32), mk_ref[...], mv_ref[...])
        o_ref[...] = _out_projection(x_ref[...], ymain_ref[...], y_mem, wo_ref)
        x_ref = o_ref
    for c in range(FFN_ROW_CHUNKS):
        part = slice(c * rows, (c + 1) * rows)
        x = x_ref[part, :]
        h = _rmsnorm(x, g_ref[...]).astype(jnp.bfloat16)
        y = x
        for start in range(0, d_ff, chunk):
            cols = slice(start, min(start + chunk, d_ff))
            gate = _dot(h, wg_ref[:, cols])
            up = _dot(h, wu_ref[:, cols])
            act = (gate / (1.0 + jnp.exp(-gate))) * up
            y = y + _dot(act.astype(jnp.bfloat16), wd_ref[cols, :])
        for k in range(n_proj):
            hk = _rmsnorm(y, proj_refs[2 * k][...]).astype(jnp.bfloat16)
            proj_out_refs[k][part, :] = _dot(hk, proj_refs[2 * k + 1][...]).astype(jnp.bfloat16)
        if final_norm:
            y = _rmsnorm(y, fg_ref[...])
        o_ref[part, :] = y


def _ffn(x, g, w_gate, w_up, w_down, final_g, final_norm, projections=(), attention=None):
    m, d = x.shape
    d_ff = w_gate[0].shape[2]
    assert d_ff % V7X_MXU_WIDTH == 0
    tm = ROW_TILE

    def row_block(n, col=0):
        return pl.BlockSpec((tm, n), lambda i: (i, col))

    attention_specs, attention_args = [], []
    if attention is not None:
        y_main, p, mkv, layer, w_o, batch = attention
        main = y_main.shape[1]
        nt = m // batch // tm
        n_mem = mkv.shape[1] // batch
        attention_specs = [row_block(main),
                           row_block(MEM_WIDTH, main // MEM_WIDTH),
                           pl.BlockSpec((None, n_mem, MEM_WIDTH), lambda i: (layer, i // nt, 0)),
                           pl.BlockSpec((None, n_mem, MEM_WIDTH), lambda i: (layer, i // nt, 1)),
                           _layer_spec(w_o)]
        attention_args = [y_main, p, mkv, mkv, w_o[0]]
    proj_widths = [w[0].shape[2] for _, w in projections]
    outs = pl.pallas_call(
        partial(_ffn_kernel, attention_out=attention is not None, final_norm=final_norm,
                n_proj=len(projections)),
        grid=(m // tm,),
        in_specs=[row_block(d),
                  _layer_spec(g),
                  _layer_spec(w_gate),
                  _layer_spec(w_up),
                  _layer_spec(w_down),
                  _layer_spec(final_g)] + attention_specs
        + [_layer_spec(q) for pair in projections for q in pair],
        out_specs=[row_block(d)] + [row_block(n) for n in proj_widths],
        out_shape=[jax.ShapeDtypeStruct((m, d), jnp.float32)]
        + [jax.ShapeDtypeStruct((m, n), jnp.bfloat16) for n in proj_widths],
        compiler_params=_params(("parallel",)),
        name="ffn",
    )(x, g[0], w_gate[0], w_up[0], w_down[0], final_g[0], *attention_args,
      *[q[0] for pair in projections for q in pair])
    return outs[0], outs[1:]


def kernel(x, mem, mix_norm, a_in, conv_w, b_in, kv_norm, w_kv_shared, w_mem_kv, w_o, ffn_norm,
           w_gate, w_up, w_down, mem_norm, final_norm):
    batch, seq, d = x.shape
    depth = mix_norm.shape[0]
    n_a = a_in.shape[0]
    main = conv_w.shape[2]
    bf16 = jnp.bfloat16
    assert seq % ROW_TILE == 0 and seq % (SB_TILE * SB_Q_TILES) == 0 and main % V7X_LANES == 0
    assert SB_Q_TILES >= SB_FUSED_BLOCKS - 1
    assert 1 <= n_a < depth

    xs = x.reshape(batch * seq, d)

    def gains(g):
        return g.reshape(-1, 1, d)

    mix_g, ffn_g = gains(mix_norm), gains(ffn_norm)
    w_in_a, w_in_b, wo = a_in.astype(bf16), b_in.astype(bf16), w_o.astype(bf16)
    wg, wu, wd = w_gate.astype(bf16), w_up.astype(bf16), w_down.astype(bf16)
    w_kv = w_kv_shared.astype(bf16)[None]
    mkv = _memory_kv(mem.reshape(-1, d), (gains(mem_norm), 0), w_mem_kv.astype(bf16))

    kv = p = None
    for i in range(depth):
        attention = None
        if i < n_a:
            xs = _conv_mixer(xs, (mix_g, i), (w_in_a, i), (conv_w, i), mkv, i, (wo, i), batch)
        else:
            attention = (_sb_attention(p, kv, batch, main), p, mkv, i, (wo, i), batch)
        projections = []
        if i == n_a - 1:
            projections.append(((gains(kv_norm), 0), (w_kv, 0)))
        if n_a - 1 <= i < depth - 1:
            projections.append(((mix_g, i + 1), (w_in_b, i + 1 - n_a)))
        xs, extra = _ffn(xs, (ffn_g, i), (wg, i), (wu, i), (wd, i), (gains(final_norm), 0),
                         i == depth - 1, projections, attention)
        if i == n_a - 1:
            kv = extra[0]
        if projections:
            p = extra[-1]
    return xs.reshape(batch, seq, d)
```

```python
from functools import partial

import jax
import jax.numpy as jnp
from jax import lax
from jax.experimental import pallas as pl
from jax.experimental.pallas import tpu as pltpu

HEAD_DIM = 64
MEM_HEADS = 4
MEM_WIDTH = MEM_HEADS * HEAD_DIM
CONV_WIDTH = 3
EPS = 1e-6

V7X_LANES = 128
V7X_SUBLANES = 8
V7X_MXU_WIDTH = 256
V7X_VMEM_LIMIT_BYTES = 56 * 1024 * 1024

ROW_TILE = 1024
SB_TILE = 128
SB_Q_TILES = 8
SB_FUSED_BLOCKS = 3
FFN_CHUNKS = 2
FFN_ROW_CHUNKS = 2

_NT = (((1,), (1,)), ((), ()))


def _rmsnorm(x, g):
    y = x * lax.rsqrt(jnp.mean(x * x, axis=-1, keepdims=True) + EPS)
    return y * g


def _dot(a, b):
    return jnp.dot(a, b, preferred_element_type=jnp.float32)


def _layer_spec(param):
    stacked, layer = param
    shape = stacked.shape[1:]
    return pl.BlockSpec((None,) + shape, lambda *_: (layer,) + (0,) * len(shape),
                        pipeline_mode=pl.Buffered(1))


def _params(semantics):
    return pltpu.CompilerParams(dimension_semantics=semantics,
                                vmem_limit_bytes=V7X_VMEM_LIMIT_BYTES)


def _norm_matmul_kernel(x_ref, g_ref, w_ref, o_ref):
    h = _rmsnorm(x_ref[...], g_ref[...]).astype(jnp.bfloat16)
    o_ref[...] = _dot(h, w_ref[...]).astype(o_ref.dtype)


def _memory_kv(mem, g, w):
    m, d = mem.shape
    layers, _, n = w.shape
    return pl.pallas_call(
        _norm_matmul_kernel,
        grid=(layers,),
        in_specs=[pl.BlockSpec((m, d), lambda l: (0, 0)),
                  _layer_spec(g),
                  pl.BlockSpec((None, d, n), lambda l: (l, 0, 0))],
        out_specs=pl.BlockSpec((None, m, n), lambda l: (l, 0, 0)),
        out_shape=jax.ShapeDtypeStruct((layers, m, n), jnp.bfloat16),
        compiler_params=_params(("parallel",)),
        name="memory_kv",
    )(mem, g[0], w)


def _memory_attention(q_mem, mk, mv):
    lane = lax.broadcasted_iota(jnp.int32, (1, MEM_WIDTH), 1)
    scale = 1.0 / (HEAD_DIM ** 0.5)
    out = jnp.zeros(q_mem.shape, jnp.float32)
    for h in range(MEM_HEADS):
        in_head = (lane >= h * HEAD_DIM) & (lane < (h + 1) * HEAD_DIM)
        qh = jnp.where(in_head, q_mem, 0.0).astype(jnp.bfloat16)
        s = lax.dot_general(qh, mk, _NT, preferred_element_type=jnp.float32) * scale
        p = jnp.exp(s - jnp.max(s, axis=-1, keepdims=True))
        l = jnp.sum(p, axis=-1, keepdims=True)
        vh = jnp.where(in_head, mv, jnp.zeros_like(mv))
        out = out + _dot(p.astype(jnp.bfloat16), vh) / l
    return out


def _out_projection(x, y_main, y_mem, wo_ref):
    main_width = y_main.shape[1]
    y = _dot(y_main.astype(jnp.bfloat16), wo_ref[:main_width, :])
    y = y + _dot(y_mem.astype(jnp.bfloat16), wo_ref[main_width:, :])
    return x + y


def _conv_mixer_kernel(x_ref, g_ref, win_ref, cw_ref, mk_ref, mv_ref, wo_ref, o_ref, tail_ref):
    main = cw_ref.shape[1]
    x = x_ref[...]
    h = _rmsnorm(x, g_ref[...]).astype(jnp.bfloat16)
    p = _dot(h, win_ref[...])
    b_gate = p[:, :main]
    cu = p[:, main:2 * main] * p[:, 2 * main:3 * main]
    q_mem = p[:, 3 * main:]

    @pl.when(pl.program_id(1) == 0)
    def _():
        tail_ref[...] = jnp.zeros_like(tail_ref)

    prev = tail_ref[...]
    rows = lax.broadcasted_iota(jnp.int32, prev.shape, 0)
    conv = cu * cw_ref[CONV_WIDTH - 1:CONV_WIDTH, :]
    for back in range(1, CONV_WIDTH):
        shifted = pltpu.roll(cu, back, 0)
        top = jnp.where(rows < back, pltpu.roll(prev, back, 0), shifted[:V7X_SUBLANES])
        shifted = jnp.concatenate([top, shifted[V7X_SUBLANES:]], axis=0)
        conv = conv + shifted * cw_ref[CONV_WIDTH - 1 - back:CONV_WIDTH - back, :]
    tail_ref[...] = cu[cu.shape[0] - V7X_SUBLANES:]

    y_main = b_gate * conv
    y_mem = _memory_attention(q_mem, mk_ref[...], mv_ref[...])
    o_ref[...] = _out_projection(x, y_main, y_mem, wo_ref)


def _conv_mixer(x, g, w_in, conv_w, mkv, layer, w_o, batch):
    m, d = x.shape
    main = conv_w[0].shape[2]
    seq = m // batch
    tm = ROW_TILE
    nt = seq // tm
    n_mem = mkv.shape[1] // batch
    return pl.pallas_call(
        _conv_mixer_kernel,
        grid=(batch, nt),
        in_specs=[pl.BlockSpec((tm, d), lambda b, t: (b * nt + t, 0)),
                  _layer_spec(g),
                  _layer_spec(w_in),
                  _layer_spec(conv_w),
                  pl.BlockSpec((None, n_mem, MEM_WIDTH), lambda b, t: (layer, b, 0)),
                  pl.BlockSpec((None, n_mem, MEM_WIDTH), lambda b, t: (layer, b, 1)),
                  _layer_spec(w_o)],
        out_specs=pl.BlockSpec((tm, d), lambda b, t: (b * nt + t, 0)),
        out_shape=jax.ShapeDtypeStruct((m, d), jnp.float32),
        scratch_shapes=[pltpu.VMEM((V7X_SUBLANES, main), jnp.float32)],
        compiler_params=_params(("arbitrary", "arbitrary")),
        name="conv_mixer",
    )(x, g[0], w_in[0], conv_w[0], mkv, mkv, w_o[0])


LOG2E = 1.4426950408889634
MASKED_SCORE = -1e30


def _split_bf16(x):
    hi = x.astype(jnp.bfloat16)
    lo = (x - hi.astype(jnp.float32)).astype(jnp.bfloat16)
    return jnp.concatenate([hi, lo], axis=1)


SB_DONE_LOG2 = 160.0


def _sb_attention_kernel(q_ref, k_ref, v_ref, o_ref):
    tk = SB_TILE
    heads = range(V7X_LANES // HEAD_DIM)
    n_sub = q_ref.shape[0] // tk
    first_tile = pl.program_id(2) * n_sub
    lane = lax.broadcasted_iota(jnp.int32, (1, V7X_LANES), 1)
    in_head = [(lane >= h * HEAD_DIM) & (lane < (h + 1) * HEAD_DIM) for h in heads]
    r = lax.broadcasted_iota(jnp.int32, (tk, tk), 0)
    s = lax.broadcasted_iota(jnp.int32, (tk, tk), 1)
    suffix = jnp.where(r >= s, 1.0, 0.0).astype(jnp.bfloat16)
    suffix2 = jnp.concatenate([suffix, suffix], axis=0)
    strictly_before = r > s

    def queries(u):
        q = q_ref[u * tk:(u + 1) * tk, :]
        return [jnp.where(m, q, jnp.zeros_like(q)) * (HEAD_DIM ** -0.5) for m in in_head]

    def sweep(jobs):
        chains = [(u, n, h) for u, job in enumerate(jobs) for n in range(len(job[1])) for h in heads]
        kv = {}
        for u, job in enumerate(jobs):
            for n, (j, _) in enumerate(job[1]):
                rows = pl.ds(pl.multiple_of(j * tk, tk), tk)
                kv[u, n] = (k_ref[rows, :], v_ref[rows, :])
        y = {}
        for u, n, h in chains:
            yc = lax.dot_general(jobs[u][0][h], kv[u, n][0], _NT,
                                 preferred_element_type=jnp.float32) * LOG2E
            y[u, n, h] = jnp.where(strictly_before, yc, MASKED_SCORE) if jobs[u][1][n][1] else yc
        p = {c: jnp.maximum(y[c], 0.0) + jnp.log2(1.0 + jnp.exp2(-jnp.abs(y[c]))) for c in chains}
        total = {}
        c = [job[2] for job in jobs]
        for n in range(max(len(job[1]) for job in jobs)):
            for u, job in enumerate(jobs):
                if n >= len(job[1]):
                    continue
                for h in heads:
                    t = _dot(_split_bf16(p[u, n, h]), suffix2)
                    total[u, n, h] = t if c[u] is None else t + c[u][h]
                c[u] = [total[u, n, h][:, 0:1] for h in heads]
        acc = [list(job[3]) for job in jobs]
        for u, n, h in chains:
            a = jnp.exp2(y[u, n, h] - total[u, n, h])
            acc[u][h] = acc[u][h] + _dot(a.astype(jnp.bfloat16), kv[u, n][1])
        return [(jnp.minimum(jnp.min(c[u][0]), jnp.min(c[u][1])), c[u], acc[u])
                for u in range(len(jobs))]

    def store(u, acc):
        o_ref[u * tk:(u + 1) * tk, :] = jnp.where(in_head[0], acc[0], acc[1]).astype(o_ref.dtype)

    def finish(u, qh, next_block, state):
        def unfinished(state):
            return (state[0] >= 0) & (state[1] < SB_DONE_LOG2)

        def body(state):
            j, _, c, acc = state
            return (j - 1,) + sweep([(qh, [(j, False)], c, acc)])[0]

        store(u, lax.while_loop(unfinished, body, (next_block,) + state)[3])

    zeros = [jnp.zeros((tk, V7X_LANES), jnp.float32) for _ in heads]
    qs = [queries(u) for u in range(n_sub)]

    def first_blocks(tile, count):
        return [(tile, True)] + [(tile - i, False) for i in range(1, count)]

    @pl.when(first_tile == 0)
    def _():
        jobs = [(qs[u], first_blocks(u, min(u + 1, SB_FUSED_BLOCKS)), None, zeros)
                for u in range(n_sub)]
        for u, state in enumerate(sweep(jobs)):
            if u < SB_FUSED_BLOCKS:
                store(u, state[2])
            else:
                finish(u, qs[u], u - SB_FUSED_BLOCKS, state)

    @pl.when(first_tile > 0)
    def _():
        jobs = [(qs[u], first_blocks(first_tile + u, SB_FUSED_BLOCKS), None, zeros)
                for u in range(n_sub)]
        for u, state in enumerate(sweep(jobs)):
            finish(u, qs[u], first_tile + u - SB_FUSED_BLOCKS, state)


def _sb_attention(p, kv, batch, main):
    m = p.shape[0]
    seq = m // batch
    tq = SB_TILE * SB_Q_TILES
    nq = seq // tq
    n_pairs = main // V7X_LANES
    return pl.pallas_call(
        _sb_attention_kernel,
        grid=(batch, n_pairs, nq),
        in_specs=[pl.BlockSpec((tq, V7X_LANES), lambda b, hp, qi: (b * nq + qi, hp)),
                  pl.BlockSpec((seq, V7X_LANES), lambda b, hp, qi: (b, hp)),
                  pl.BlockSpec((seq, V7X_LANES), lambda b, hp, qi: (b, n_pairs + hp))],
        out_specs=pl.BlockSpec((tq, V7X_LANES), lambda b, hp, qi: (b * nq + qi, hp)),
        out_shape=jax.ShapeDtypeStruct((m, main), jnp.bfloat16),
        compiler_params=_params(("parallel", "parallel", "parallel")),
        name="sb_attention",
    )(p, kv, kv)


def _ffn_kernel(*refs, attention_out, final_norm, n_proj):
    x_ref, g_ref, wg_ref, wu_ref, wd_ref, fg_ref = refs[:6]
    refs = refs[6:]
    if attention_out:
        ymain_ref, qmem_ref, mk_ref, mv_ref, wo_ref = refs[:5]
        refs = refs[5:]
    proj_refs = refs[:2 * n_proj]
    o_ref = refs[2 * n_proj]
    proj_out_refs = refs[2 * n_proj + 1:]
    d_ff = wg_ref.shape[1]
    chunk = pl.cdiv(d_ff // V7X_MXU_WIDTH, FFN_CHUNKS) * V7X_MXU_WIDTH
    rows = x_ref.shape[0] // FFN_ROW_CHUNKS
    if attention_out:
        y_mem = _memory_attention(qmem_ref[...].astype(jnp.float32), mk_ref[...], mv_ref[...])
        o_ref[...] = _out_projection(x_ref[...], ymain_ref[...], y_mem, wo_ref)
        x_ref = o_ref
    for c in range(FFN_ROW_CHUNKS):
        part = slice(c * rows, (c + 1) * rows)
        x = x_ref[part, :]
        h = _rmsnorm(x, g_ref[...]).astype(jnp.bfloat16)
        y = x
        for start in range(0, d_ff, chunk):
            cols = slice(start, min(start + chunk, d_ff))
            gate = _dot(h, wg_ref[:, cols])
            up = _dot(h, wu_ref[:, cols])
            act = (gate / (1.0 + jnp.exp(-gate))) * up
            y = y + _dot(act.astype(jnp.bfloat16), wd_ref[cols, :])
        for k in range(n_proj):
            hk = _rmsnorm(y, proj_refs[2 * k][...]).astype(jnp.bfloat16)
            proj_out_refs[k][part, :] = _dot(hk, proj_refs[2 * k + 1][...]).astype(jnp.bfloat16)
        if final_norm:
            y = _rmsnorm(y, fg_ref[...])
        o_ref[part, :] = y


def _ffn(x, g, w_gate, w_up, w_down, final_g, final_norm, projections=(), attention=None):
    m, d = x.shape
    d_ff = w_gate[0].shape[2]
    assert d_ff % V7X_MXU_WIDTH == 0
    tm = ROW_TILE

    def row_block(n, col=0):
        return pl.BlockSpec((tm, n), lambda i: (i, col))

    attention_specs, attention_args = [], []
    if attention is not None:
        y_main, p, mkv, layer, w_o, batch = attention
        main = y_main.shape[1]
        nt = m // batch // tm
        n_mem = mkv.shape[1] // batch
        attention_specs = [row_block(main),
                           row_block(MEM_WIDTH, main // MEM_WIDTH),
                           pl.BlockSpec((None, n_mem, MEM_WIDTH), lambda i: (layer, i // nt, 0)),
                           pl.BlockSpec((None, n_mem, MEM_WIDTH), lambda i: (layer, i // nt, 1)),
                           _layer_spec(w_o)]
        attention_args = [y_main, p, mkv, mkv, w_o[0]]
    proj_widths = [w[0].shape[2] for _, w in projections]
    outs = pl.pallas_call(
        partial(_ffn_kernel, attention_out=attention is not None, final_norm=final_norm,
                n_proj=len(projections)),
        grid=(m // tm,),
        in_specs=[row_block(d),
                  _layer_spec(g),
                  _layer_spec(w_gate),
                  _layer_spec(w_up),
                  _layer_spec(w_down),
                  _layer_spec(final_g)] + attention_specs
        + [_layer_spec(q) for pair in projections for q in pair],
        out_specs=[row_block(d)] + [row_block(n) for n in proj_widths],
        out_shape=[jax.ShapeDtypeStruct((m, d), jnp.float32)]
        + [jax.ShapeDtypeStruct((m, n), jnp.bfloat16) for n in proj_widths],
        compiler_params=_params(("parallel",)),
        name="ffn",
    )(x, g[0], w_gate[0], w_up[0], w_down[0], final_g[0], *attention_args,
      *[q[0] for pair in projections for q in pair])
    return outs[0], outs[1:]


def kernel(x, mem, mix_norm, a_in, conv_w, b_in, kv_norm, w_kv_shared, w_mem_kv, w_o, ffn_norm,
           w_gate, w_up, w_down, mem_norm, final_norm):
    batch, seq, d = x.shape
    depth = mix_norm.shape[0]
    n_a = a_in.shape[0]
    main = conv_w.shape[2]
    bf16 = jnp.bfloat16
    assert seq % ROW_TILE == 0 and seq % (SB_TILE * SB_Q_TILES) == 0 and main % V7X_LANES == 0
    assert SB_Q_TILES >= SB_FUSED_BLOCKS - 1
    assert 1 <= n_a < depth

    xs = x.reshape(batch * seq, d)

    def gains(g):
        return g.reshape(-1, 1, d)

    mix_g, ffn_g = gains(mix_norm), gains(ffn_norm)
    w_in_a, w_in_b, wo = a_in.astype(bf16), b_in.astype(bf16), w_o.astype(bf16)
    wg, wu, wd = w_gate.astype(bf16), w_up.astype(bf16), w_down.astype(bf16)
    w_kv = w_kv_shared.astype(bf16)[None]
    mkv = _memory_kv(mem.reshape(-1, d), (gains(mem_norm), 0), w_mem_kv.astype(bf16))

    kv = p = None
    for i in range(depth):
        attention = None
        if i < n_a:
            xs = _conv_mixer(xs, (mix_g, i), (w_in_a, i), (conv_w, i), mkv, i, (wo, i), batch)
        else:
            attention = (_sb_attention(p, kv, batch, main), p, mkv, i, (wo, i), batch)
        projections = []
        if i == n_a - 1:
            projections.append(((gains(kv_norm), 0), (w_kv, 0)))
        if n_a - 1 <= i < depth - 1:
            projections.append(((mix_g, i + 1), (w_in_b, i + 1 - n_a)))
        xs, extra = _ffn(xs, (ffn_g, i), (wg, i), (wu, i), (wd, i), (gains(final_norm), 0),
                         i == depth - 1, projections, attention)
        if i == n_a - 1:
            kv = extra[0]
        if projections:
            p = extra[-1]
    return xs.reshape(batch, seq, d)
```

```python
from functools import partial

import jax
import jax.numpy as jnp
from jax import lax
from jax.experimental import pallas as pl
from jax.experimental.pallas import tpu as pltpu

HEAD_DIM = 64
MEM_HEADS = 4
MEM_WIDTH = MEM_HEADS * HEAD_DIM
CONV_WIDTH = 3
EPS = 1e-6

V7X_LANES = 128
V7X_SUBLANES = 8
V7X_MXU_WIDTH = 256
V7X_VMEM_LIMIT_BYTES = 56 * 1024 * 1024

ROW_TILE = 1024
SB_TILE = 128
SB_Q_TILES = 16
SB_FUSED_BLOCKS = 3
FFN_CHUNKS = 2
FFN_ROW_CHUNKS = 2

_NT = (((1,), (1,)), ((), ()))


def _rmsnorm(x, g):
    y = x * lax.rsqrt(jnp.mean(x * x, axis=-1, keepdims=True) + EPS)
    return y * g


def _dot(a, b):
    return jnp.dot(a, b, preferred_element_type=jnp.float32)


def _layer_spec(param):
    stacked, layer = param
    shape = stacked.shape[1:]
    return pl.BlockSpec((None,) + shape, lambda *_: (layer,) + (0,) * len(shape),
                        pipeline_mode=pl.Buffered(1))


def _params(semantics):
    return pltpu.CompilerParams(dimension_semantics=semantics,
                                vmem_limit_bytes=V7X_VMEM_LIMIT_BYTES)


def _norm_matmul_kernel(x_ref, g_ref, w_ref, o_ref):
    h = _rmsnorm(x_ref[...], g_ref[...]).astype(jnp.bfloat16)
    o_ref[...] = _dot(h, w_ref[...]).astype(o_ref.dtype)


def _memory_kv(mem, g, w):
    m, d = mem.shape
    layers, _, n = w.shape
    return pl.pallas_call(
        _norm_matmul_kernel,
        grid=(layers,),
        in_specs=[pl.BlockSpec((m, d), lambda l: (0, 0)),
                  _layer_spec(g),
                  pl.BlockSpec((None, d, n), lambda l: (l, 0, 0))],
        out_specs=pl.BlockSpec((None, m, n), lambda l: (l, 0, 0)),
        out_shape=jax.ShapeDtypeStruct((layers, m, n), jnp.bfloat16),
        compiler_params=_params(("parallel",)),
        name="memory_kv",
    )(mem, g[0], w)


def _memory_attention(q_mem, mk, mv):
    lane = lax.broadcasted_iota(jnp.int32, (1, MEM_WIDTH), 1)
    scale = 1.0 / (HEAD_DIM ** 0.5)
    out = jnp.zeros(q_mem.shape, jnp.float32)
    for h in range(MEM_HEADS):
        in_head = (lane >= h * HEAD_DIM) & (lane < (h + 1) * HEAD_DIM)
        qh = jnp.where(in_head, q_mem, 0.0).astype(jnp.bfloat16)
        s = lax.dot_general(qh, mk, _NT, preferred_element_type=jnp.float32) * scale
        p = jnp.exp(s - jnp.max(s, axis=-1, keepdims=True))
        l = jnp.sum(p, axis=-1, keepdims=True)
        vh = jnp.where(in_head, mv, jnp.zeros_like(mv))
        out = out + _dot(p.astype(jnp.bfloat16), vh) / l
    return out


def _out_projection(x, y_main, y_mem, wo_ref):
    main_width = y_main.shape[1]
    y = _dot(y_main.astype(jnp.bfloat16), wo_ref[:main_width, :])
    y = y + _dot(y_mem.astype(jnp.bfloat16), wo_ref[main_width:, :])
    return x + y


def _conv_mixer_kernel(x_ref, g_ref, win_ref, cw_ref, mk_ref, mv_ref, wo_ref, o_ref, tail_ref):
    main = cw_ref.shape[1]
    x = x_ref[...]
    h = _rmsnorm(x, g_ref[...]).astype(jnp.bfloat16)
    p = _dot(h, win_ref[...])
    b_gate = p[:, :main]
    cu = p[:, main:2 * main] * p[:, 2 * main:3 * main]
    q_mem = p[:, 3 * main:]

    @pl.when(pl.program_id(1) == 0)
    def _():
        tail_ref[...] = jnp.zeros_like(tail_ref)

    prev = tail_ref[...]
    rows = lax.broadcasted_iota(jnp.int32, prev.shape, 0)
    conv = cu * cw_ref[CONV_WIDTH - 1:CONV_WIDTH, :]
    for back in range(1, CONV_WIDTH):
        shifted = pltpu.roll(cu, back, 0)
        top = jnp.where(rows < back, pltpu.roll(prev, back, 0), shifted[:V7X_SUBLANES])
        shifted = jnp.concatenate([top, shifted[V7X_SUBLANES:]], axis=0)
        conv = conv + shifted * cw_ref[CONV_WIDTH - 1 - back:CONV_WIDTH - back, :]
    tail_ref[...] = cu[cu.shape[0] - V7X_SUBLANES:]

    y_main = b_gate * conv
    y_mem = _memory_attention(q_mem, mk_ref[...], mv_ref[...])
    o_ref[...] = _out_projection(x, y_main, y_mem, wo_ref)


def _conv_mixer(x, g, w_in, conv_w, mkv, layer, w_o, batch):
    m, d = x.shape
    main = conv_w[0].shape[2]
    seq = m // batch
    tm = ROW_TILE
    nt = seq // tm
    n_mem = mkv.shape[1] // batch
    return pl.pallas_call(
        _conv_mixer_kernel,
        grid=(batch, nt),
        in_specs=[pl.BlockSpec((tm, d), lambda b, t: (b * nt + t, 0)),
                  _layer_spec(g),
                  _layer_spec(w_in),
                  _layer_spec(conv_w),
                  pl.BlockSpec((None, n_mem, MEM_WIDTH), lambda b, t: (layer, b, 0)),
                  pl.BlockSpec((None, n_mem, MEM_WIDTH), lambda b, t: (layer, b, 1)),
                  _layer_spec(w_o)],
        out_specs=pl.BlockSpec((tm, d), lambda b, t: (b * nt + t, 0)),
        out_shape=jax.ShapeDtypeStruct((m, d), jnp.float32),
        scratch_shapes=[pltpu.VMEM((V7X_SUBLANES, main), jnp.float32)],
        compiler_params=_params(("arbitrary", "arbitrary")),
        name="conv_mixer",
    )(x, g[0], w_in[0], conv_w[0], mkv, mkv, w_o[0])


LOG2E = 1.4426950408889634
MASKED_SCORE = -1e30


def _split_bf16(x):
    hi = x.astype(jnp.bfloat16)
    lo = (x - hi.astype(jnp.float32)).astype(jnp.bfloat16)
    return jnp.concatenate([hi, lo], axis=1)


SB_DONE_LOG2 = 160.0


def _sb_attention_kernel(q_ref, k_ref, v_ref, o_ref):
    tk = SB_TILE
    heads = range(V7X_LANES // HEAD_DIM)
    n_sub = q_ref.shape[0] // tk
    first_tile = pl.program_id(2) * n_sub
    lane = lax.broadcasted_iota(jnp.int32, (1, V7X_LANES), 1)
    in_head = [(lane >= h * HEAD_DIM) & (lane < (h + 1) * HEAD_DIM) for h in heads]
    r = lax.broadcasted_iota(jnp.int32, (tk, tk), 0)
    s = lax.broadcasted_iota(jnp.int32, (tk, tk), 1)
    suffix = jnp.where(r >= s, 1.0, 0.0).astype(jnp.bfloat16)
    suffix2 = jnp.concatenate([suffix, suffix], axis=0)
    strictly_before = r > s

    def queries(u):
        q = q_ref[u * tk:(u + 1) * tk, :]
        return [jnp.where(m, q, jnp.zeros_like(q)) * (HEAD_DIM ** -0.5) for m in in_head]

    def sweep(jobs):
        chains = [(u, n, h) for u, job in enumerate(jobs) for n in range(len(job[1])) for h in heads]
        kv = {}
        for u, job in enumerate(jobs):
            for n, (j, _) in enumerate(job[1]):
                rows = pl.ds(pl.multiple_of(j * tk, tk), tk)
                kv[u, n] = (k_ref[rows, :], v_ref[rows, :])
        y = {}
        for u, n, h in chains:
            yc = lax.dot_general(jobs[u][0][h], kv[u, n][0], _NT,
                                 preferred_element_type=jnp.float32) * LOG2E
            y[u, n, h] = jnp.where(strictly_before, yc, MASKED_SCORE) if jobs[u][1][n][1] else yc
        p = {c: jnp.maximum(y[c], 0.0) + jnp.log2(1.0 + jnp.exp2(-jnp.abs(y[c]))) for c in chains}
        total = {}
        c = [job[2] for job in jobs]
        for n in range(max(len(job[1]) for job in jobs)):
            for u, job in enumerate(jobs):
                if n >= len(job[1]):
                    continue
                for h in heads:
                    t = _dot(_split_bf16(p[u, n, h]), suffix2)
                    total[u, n, h] = t if c[u] is None else t + c[u][h]
                c[u] = [total[u, n, h][:, 0:1] for h in heads]
        acc = [list(job[3]) for job in jobs]
        for u, n, h in chains:
            a = jnp.exp2(y[u, n, h] - total[u, n, h])
            acc[u][h] = acc[u][h] + _dot(a.astype(jnp.bfloat16), kv[u, n][1])
        return [(jnp.minimum(jnp.min(c[u][0]), jnp.min(c[u][1])), c[u], acc[u])
                for u in range(len(jobs))]

    def store(u, acc):
        o_ref[u * tk:(u + 1) * tk, :] = jnp.where(in_head[0], acc[0], acc[1]).astype(o_ref.dtype)

    def finish(u, qh, next_block, state):
        def unfinished(state):
            return (state[0] >= 0) & (state[1] < SB_DONE_LOG2)

        def body(state):
            j, _, c, acc = state
            return (j - 1,) + sweep([(qh, [(j, False)], c, acc)])[0]

        store(u, lax.while_loop(unfinished, body, (next_block,) + state)[3])

    zeros = [jnp.zeros((tk, V7X_LANES), jnp.float32) for _ in heads]
    qs = [queries(u) for u in range(n_sub)]

    def first_blocks(tile, count):
        return [(tile, True)] + [(tile - i, False) for i in range(1, count)]

    @pl.when(first_tile == 0)
    def _():
        jobs = [(qs[u], first_blocks(u, min(u + 1, SB_FUSED_BLOCKS)), None, zeros)
                for u in range(n_sub)]
        for u, state in enumerate(sweep(jobs)):
            if u < SB_FUSED_BLOCKS:
                store(u, state[2])
            else:
                finish(u, qs[u], u - SB_FUSED_BLOCKS, state)

    @pl.when(first_tile > 0)
    def _():
        jobs = [(qs[u], first_blocks(first_tile + u, SB_FUSED_BLOCKS), None, zeros)
                for u in range(n_sub)]
        for u, state in enumerate(sweep(jobs)):
            finish(u, qs[u], first_tile + u - SB_FUSED_BLOCKS, state)


def _sb_attention(p, kv, batch, main):
    m = p.shape[0]
    seq = m // batch
    tq = SB_TILE * SB_Q_TILES
    nq = seq // tq
    n_pairs = main // V7X_LANES
    return pl.pallas_call(
        _sb_attention_kernel,
        grid=(batch, n_pairs, nq),
        in_specs=[pl.BlockSpec((tq, V7X_LANES), lambda b, hp, qi: (b * nq + qi, hp)),
                  pl.BlockSpec((seq, V7X_LANES), lambda b, hp, qi: (b, hp)),
                  pl.BlockSpec((seq, V7X_LANES), lambda b, hp, qi: (b, n_pairs + hp))],
        out_specs=pl.BlockSpec((tq, V7X_LANES), lambda b, hp, qi: (b * nq + qi, hp)),
        out_shape=jax.ShapeDtypeStruct((m, main), jnp.bfloat16),
        compiler_params=_params(("parallel", "parallel", "parallel")),
        name="sb_attention",
    )(p, kv, kv)


def _ffn_kernel(*refs, attention_out, final_norm, n_proj):
    x_ref, g_ref, wg_ref, wu_ref, wd_ref, fg_ref = refs[:6]
    refs = refs[6:]
    if attention_out:
        ymain_ref, qmem_ref, mk_ref, mv_ref, wo_ref = refs[:5]
        refs = refs[5:]
    proj_refs = refs[:2 * n_proj]
    o_ref = refs[2 * n_proj]
    proj_out_refs = refs[2 * n_proj + 1:]
    d_ff = wg_ref.shape[1]
    chunk = pl.cdiv(d_ff // V7X_MXU_WIDTH, FFN_CHUNKS) * V7X_MXU_WIDTH
    rows = x_ref.shape[0] // FFN_ROW_CHUNKS
    if attention_out:
        y_mem = _memory_attention(qmem_ref[...].astype(jnp.float32), mk_ref[...], mv_ref[...])
        o_ref[...] = _out_projection(x_ref[...], ymain_ref[...], y_mem, wo_ref)
        x_ref = o_ref
    for c in range(FFN_ROW_CHUNKS):
        part = slice(c * rows, (c + 1) * rows)
        x = x_ref[part, :]
        h = _rmsnorm(x, g_ref[...]).astype(jnp.bfloat16)
        y = x
        for start in range(0, d_ff, chunk):
            cols = slice(start, min(start + chunk, d_ff))
            gate = _dot(h, wg_ref[:, cols])
            up = _dot(h, wu_ref[:, cols])
            act = (gate / (1.0 + jnp.exp(-gate))) * up
            y = y + _dot(act.astype(jnp.bfloat16), wd_ref[cols, :])
        for k in range(n_proj):
            hk = _rmsnorm(y, proj_refs[2 * k][...]).astype(jnp.bfloat16)
            proj_out_refs[k][part, :] = _dot(hk, proj_refs[2 * k + 1][...]).astype(jnp.bfloat16)
        if final_norm:
            y = _rmsnorm(y, fg_ref[...])
        o_ref[part, :] = y


def _ffn(x, g, w_gate, w_up, w_down, final_g, final_norm, projections=(), attention=None):
    m, d = x.shape
    d_ff = w_gate[0].shape[2]
    assert d_ff % V7X_MXU_WIDTH == 0
    tm = ROW_TILE

    def row_block(n, col=0):
        return pl.BlockSpec((tm, n), lambda i: (i, col))

    attention_specs, attention_args = [], []
    if attention is not None:
        y_main, p, mkv, layer, w_o, batch = attention
        main = y_main.shape[1]
        nt = m // batch // tm
        n_mem = mkv.shape[1] // batch
        attention_specs = [row_block(main),
                           row_block(MEM_WIDTH, main // MEM_WIDTH),
                           pl.BlockSpec((None, n_mem, MEM_WIDTH), lambda i: (layer, i // nt, 0)),
                           pl.BlockSpec((None, n_mem, MEM_WIDTH), lambda i: (layer, i // nt, 1)),
                           _layer_spec(w_o)]
        attention_args = [y_main, p, mkv, mkv, w_o[0]]
    proj_widths = [w[0].shape[2] for _, w in projections]
    outs = pl.pallas_call(
        partial(_ffn_kernel, attention_out=attention is not None, final_norm=final_norm,
                n_proj=len(projections)),
        grid=(m // tm,),
        in_specs=[row_block(d),
                  _layer_spec(g),
                  _layer_spec(w_gate),
                  _layer_spec(w_up),
                  _layer_spec(w_down),
                  _layer_spec(final_g)] + attention_specs
        + [_layer_spec(q) for pair in projections for q in pair],
        out_specs=[row_block(d)] + [row_block(n) for n in proj_widths],
        out_shape=[jax.ShapeDtypeStruct((m, d), jnp.float32)]
        + [jax.ShapeDtypeStruct((m, n), jnp.bfloat16) for n in proj_widths],
        compiler_params=_params(("parallel",)),
        name="ffn",
    )(x, g[0], w_gate[0], w_up[0], w_down[0], final_g[0], *attention_args,
      *[q[0] for pair in projections for q in pair])
    return outs[0], outs[1:]


def kernel(x, mem, mix_norm, a_in, conv_w, b_in, kv_norm, w_kv_shared, w_mem_kv, w_o, ffn_norm,
           w_gate, w_up, w_down, mem_norm, final_norm):
    batch, seq, d = x.shape
    depth = mix_norm.shape[0]
    n_a = a_in.shape[0]
    main = conv_w.shape[2]
    bf16 = jnp.bfloat16
    assert seq % ROW_TILE == 0 and seq % (SB_TILE * SB_Q_TILES) == 0 and main % V7X_LANES == 0
    assert SB_Q_TILES >= SB_FUSED_BLOCKS - 1
    assert 1 <= n_a < depth

    xs = x.reshape(batch * seq, d)

    def gains(g):
        return g.reshape(-1, 1, d)

    mix_g, ffn_g = gains(mix_norm), gains(ffn_norm)
    w_in_a, w_in_b, wo = a_in.astype(bf16), b_in.astype(bf16), w_o.astype(bf16)
    wg, wu, wd = w_gate.astype(bf16), w_up.astype(bf16), w_down.astype(bf16)
    w_kv = w_kv_shared.astype(bf16)[None]
    mkv = _memory_kv(mem.reshape(-1, d), (gains(mem_norm), 0), w_mem_kv.astype(bf16))

    kv = p = None
    for i in range(depth):
        attention = None
        if i < n_a:
            xs = _conv_mixer(xs, (mix_g, i), (w_in_a, i), (conv_w, i), mkv, i, (wo, i), batch)
        else:
            attention = (_sb_attention(p, kv, batch, main), p, mkv, i, (wo, i), batch)
        projections = []
        if i == n_a - 1:
            projections.append(((gains(kv_norm), 0), (w_kv, 0)))
        if n_a - 1 <= i < depth - 1:
            projections.append(((mix_g, i + 1), (w_in_b, i + 1 - n_a)))
        xs, extra = _ffn(xs, (ffn_g, i), (wg, i), (wu, i), (wd, i), (gains(final_norm), 0),
                         i == depth - 1, projections, attention)
        if i == n_a - 1:
            kv = extra[0]
        if projections:
            p = extra[-1]
    return xs.reshape(batch, seq, d)
```
